```python
import jax, jax.numpy as jnp
from jax import lax
import numpy as np

D_MODEL = 1024
BATCH = 8
SEQ = 2048
DEPTH = 4

CHUNK = 64
N_A = DEPTH // 2
N_B = DEPTH - N_A
GM_BLOCK = 128
A_WIDTH = 2 * D_MODEL
A_GROUPS = 8
A_GROUP_CH = A_WIDTH // A_GROUPS
B_HEADS = 8
B_HEAD_DIM = D_MODEL // B_HEADS
B_WIDTH = B_HEADS * B_HEAD_DIM
Q_BLOCK = 128
PLE_DIM = 256
EPS = 1e-6

kernel_name = "yoco_gmlp_stickbreaking_trunk"


def rmsnorm(x, g):
    xf = x.astype(jnp.float32)
    y = xf * lax.rsqrt(jnp.mean(xf * xf, axis=-1, keepdims=True) + EPS)
    return (y * g.astype(jnp.float32)).astype(x.dtype)


def layernorm(x, g, b):
    xf = x.astype(jnp.float32)
    mu = jnp.mean(xf, axis=-1, keepdims=True)
    var = jnp.mean(jnp.square(xf - mu), axis=-1, keepdims=True)
    y = (xf - mu) * lax.rsqrt(var + EPS)
    return (y * g.astype(jnp.float32) + b.astype(jnp.float32)).astype(x.dtype)


def gmlp_mixer(hn, w_in, ln_g, ln_b, w_s, b_s, w_out):
    B, S, _ = hn.shape
    z = hn @ w_in
    u, v, gate = jnp.split(z, 3, axis=-1)
    u = jax.nn.gelu(u, approximate=False)
    v = layernorm(jax.nn.gelu(v, approximate=False), ln_g, ln_b)
    cid = jnp.arange(GM_BLOCK) // CHUNK
    mask = cid[None, :] <= cid[:, None]
    w = jnp.where(mask[None], w_s, jnp.zeros_like(w_s))
    vb = v.reshape(B, S // GM_BLOCK, GM_BLOCK, A_GROUPS, A_GROUP_CH)
    sv = jnp.einsum('gts,bnsgc->bntgc', w, vb) + b_s.T[None, None, :, :, None]
    sv = sv.reshape(B, S, A_WIDTH)
    return (u * sv * jax.nn.silu(gate)) @ w_out


def stick_breaking(q, k, v):
    S = q.shape[1]
    scale = 1.0 / np.sqrt(B_HEAD_DIM)
    outs = []
    for i in range(S // Q_BLOCK):
        q0 = i * Q_BLOCK
        k_end = q0 + Q_BLOCK
        qb = q[:, q0:k_end]
        kb = k[:, :k_end]
        vb = v[:, :k_end]
        z = jnp.einsum('bthd,bshd->bhts', qb, kb).astype(jnp.float32) * scale
        t_pos = q0 + jnp.arange(Q_BLOCK)
        s_pos = jnp.arange(k_end)
        causal = s_pos[None, :] < t_pos[:, None]
        log_beta = jnp.where(causal, jax.nn.log_sigmoid(z), -jnp.inf)
        log_1m = jnp.where(causal, jax.nn.log_sigmoid(-z), 0.0)
        suffix = lax.cumsum(log_1m, axis=3, reverse=True) - log_1m
        a = jnp.exp(log_beta + suffix).astype(vb.dtype)
        outs.append(jnp.einsum('bhts,bshd->bthd', a, vb))
    return jnp.concatenate(outs, axis=1)


def setup_inputs(seed: int = 0) -> dict:
    key = jax.random.key(seed)
    ks = jax.random.split(key, 20)
    f32 = jnp.float32
    nrm = lambda k, shape, s: jax.random.normal(k, shape, f32) * s
    D = D_MODEL
    return {
        "x": nrm(ks[0], (BATCH, SEQ, D), 1.0),
        "p": nrm(ks[1], (DEPTH, BATCH, SEQ, PLE_DIM), 1.0),
        "norm_g": 1.0 + nrm(ks[2], (DEPTH, D), 0.02),
        "a_w_in": nrm(ks[3], (N_A, D, 3 * A_WIDTH), D ** -0.5),
        "a_ln_g": 1.0 + nrm(ks[4], (N_A, A_WIDTH), 0.02),
        "a_ln_b": nrm(ks[5], (N_A, A_WIDTH), 0.02),
        "a_w_s": nrm(ks[6], (N_A, A_GROUPS, GM_BLOCK, GM_BLOCK), 0.5 * GM_BLOCK ** -0.5),
        "a_b_s": 1.0 + nrm(ks[7], (N_A, A_GROUPS, GM_BLOCK), 0.02),
        "a_w_out": nrm(ks[8], (N_A, A_WIDTH, D), A_WIDTH ** -0.5),
        "kv_norm_g": 1.0 + nrm(ks[9], (D,), 0.02),
        "w_kv": nrm(ks[10], (D, 2 * B_WIDTH), D ** -0.5),
        "b_w_in": nrm(ks[11], (N_B, D, 2 * B_WIDTH), D ** -0.5),
        "b_w_out": nrm(ks[12], (N_B, B_WIDTH, D), B_WIDTH ** -0.5),
        "ple_w": nrm(ks[13], (DEPTH, PLE_DIM, D), 0.5 * PLE_DIM ** -0.5),
        "ple_gate_w": nrm(ks[14], (DEPTH, D, D), D ** -0.5),
        "final_g": 1.0 + nrm(ks[15], (D,), 0.02),
    }


def reference(x, p, norm_g, a_w_in, a_ln_g, a_ln_b, a_w_s, a_b_s, a_w_out,
              kv_norm_g, w_kv, b_w_in, b_w_out, ple_w, ple_gate_w, final_g):
    B, S, _ = x.shape
    h = x
    k_sh = None
    v_sh = None
    for i in range(DEPTH):
        hn = rmsnorm(h, norm_g[i])
        if i < N_A:
            mix = gmlp_mixer(hn, a_w_in[i], a_ln_g[i], a_ln_b[i],
                             a_w_s[i], a_b_s[i], a_w_out[i])
        else:
            if k_sh is None:
                kv = rmsnorm(h, kv_norm_g) @ w_kv
                k_sh, v_sh = jnp.split(kv, 2, axis=-1)
                k_sh = k_sh.reshape(B, S, B_HEADS, B_HEAD_DIM)
                v_sh = v_sh.reshape(B, S, B_HEADS, B_HEAD_DIM)
            j = i - N_A
            qg = hn @ b_w_in[j]
            q, gate = jnp.split(qg, 2, axis=-1)
            q = q.reshape(B, S, B_HEADS, B_HEAD_DIM)
            o = stick_breaking(q, k_sh, v_sh).reshape(B, S, B_WIDTH)
            mix = (o * jax.nn.silu(gate)) @ b_w_out[j]
        h = h + mix
        h = h + jax.nn.sigmoid(h @ ple_gate_w[i]) * (p[i] @ ple_w[i])
    return rmsnorm(h, final_g)
```

```python
import functools

import jax
import jax.numpy as jnp
import numpy as np
from jax import lax
from jax.experimental import pallas as pl
from jax.experimental.pallas import tpu as pltpu

D_MODEL = 1024
DEPTH = 4
N_A = DEPTH // 2
CHUNK = 64
GM_BLOCK = 128
A_WIDTH = 2 * D_MODEL
A_GROUPS = 8
A_GROUP_CH = A_WIDTH // A_GROUPS
B_HEADS = 8
B_HEAD_DIM = D_MODEL // B_HEADS
PLE_DIM = 256
EPS = 1e-6

COLS = 256
TM_A = 256
TM_B = 512
TQ = 256
TK = 256
VMEM_LIMIT = 56 * 1024 * 1024

_SQRT_HALF = float(np.sqrt(0.5))
_F32 = jnp.float32
_BF16 = jnp.bfloat16


def _rms(x, g):
    ms = jnp.mean(x * x, axis=-1, keepdims=True)
    return x * lax.rsqrt(ms + EPS) * g


def _gelu(x):
    return 0.5 * x * (1.0 + lax.erf(x * _SQRT_HALF))


def _silu(x):
    return x * jax.nn.sigmoid(x)


def _dot(a, b):
    return jnp.dot(a, b, preferred_element_type=_F32)


def _ple_and_store(h1_ref, h1b_ref, pb, gate_w_ref, ple_w_ref, out_ref, final_g):
    h1b = h1b_ref[...]
    for c in range(D_MODEL // COLS):
        cs = slice(c * COLS, (c + 1) * COLS)
        pg = _dot(h1b, gate_w_ref[:, cs])
        pe = _dot(pb, ple_w_ref[:, cs])
        out_ref[:, cs] = h1_ref[:, cs] + jax.nn.sigmoid(pg) * pe
    if final_g is not None:
        out_ref[...] = _rms(out_ref[...], final_g)


def _a_layer_kernel(h_ref, p_ref, ng_ref, w_in_ref, lng_ref, lnb_ref, ws_ref,
                    bst_ref, w_out_ref, gate_w_ref, ple_w_ref, out_ref,
                    hn_ref, v_ref, vn_ref, y_ref, h1_ref, h1b_ref):
    tm = h_ref.shape[0]
    n_grp = tm // GM_BLOCK
    n_chunks = A_WIDTH // COLS

    hn_ref[...] = _rms(h_ref[...], ng_ref[...]).astype(_BF16)
    hn = hn_ref[...]

    s1 = jnp.zeros((tm, 1), _F32)
    for c in range(n_chunks):
        cs = slice(c * COLS, (c + 1) * COLS)
        vc = _gelu(_dot(hn, w_in_ref[:, A_WIDTH + c * COLS:A_WIDTH + (c + 1) * COLS]))
        v_ref[:, cs] = vc
        s1 = s1 + jnp.sum(vc, axis=-1, keepdims=True)
    mu = s1 * (1.0 / A_WIDTH)
    s2 = jnp.zeros((tm, 1), _F32)
    for c in range(n_chunks):
        cs = slice(c * COLS, (c + 1) * COLS)
        d = v_ref[:, cs] - mu
        s2 = s2 + jnp.sum(d * d, axis=-1, keepdims=True)
    rstd = lax.rsqrt(s2 * (1.0 / A_WIDTH) + EPS)
    for c in range(n_chunks):
        cs = slice(c * COLS, (c + 1) * COLS)
        vn = (v_ref[:, cs] - mu) * rstd * lng_ref[:, cs] + lnb_ref[:, cs]
        vn_ref[:, cs] = vn.astype(_BF16)

    t_chunk = lax.broadcasted_iota(jnp.int32, (GM_BLOCK, GM_BLOCK), 0) // CHUNK
    s_chunk = lax.broadcasted_iota(jnp.int32, (GM_BLOCK, GM_BLOCK), 1) // CHUNK
    mask = s_chunk <= t_chunk

    for g in range(A_GROUPS):
        cs = slice(g * A_GROUP_CH, (g + 1) * A_GROUP_CH)
        u = _gelu(_dot(hn, w_in_ref[:, cs]))
        gate = _dot(hn, w_in_ref[:, 2 * A_WIDTH + g * A_GROUP_CH:2 * A_WIDTH + (g + 1) * A_GROUP_CH])
        wm = jnp.where(mask, ws_ref[g], 0.0).astype(_BF16)
        b_col = bst_ref[:, g:g + 1]
        sv = jnp.concatenate(
            [_dot(wm, vn_ref[n * GM_BLOCK:(n + 1) * GM_BLOCK, cs]) + b_col
             for n in range(n_grp)], axis=0)
        y_ref[:, cs] = (u * sv * _silu(gate)).astype(_BF16)

    y = y_ref[...]
    for c in range(D_MODEL // COLS):
        cs = slice(c * COLS, (c + 1) * COLS)
        h1 = h_ref[:, cs] + _dot(y, w_out_ref[:, cs])
        h1_ref[:, cs] = h1
        h1b_ref[:, cs] = h1.astype(_BF16)
    _ple_and_store(h1_ref, h1b_ref, p_ref[...].astype(_BF16), gate_w_ref,
                   ple_w_ref, out_ref, None)


def _const_spec(shape):
    nd = len(shape)
    return pl.BlockSpec(shape, lambda i, _nd=nd: (0,) * _nd,
                        pipeline_mode=pl.Buffered(1))


def _a_layer(h, p, norm_g, w_in, ln_g, ln_b, w_s, b_s_t, w_out, gate_w, ple_w):
    t = h.shape[0]
    tm = TM_A
    return pl.pallas_call(
        _a_layer_kernel,
        grid=(t // tm,),
        in_specs=[
            pl.BlockSpec((tm, D_MODEL), lambda i: (i, 0)),
            pl.BlockSpec((tm, PLE_DIM), lambda i: (i, 0)),
            _const_spec((1, D_MODEL)),
            _const_spec((D_MODEL, 3 * A_WIDTH)),
            _const_spec((1, A_WIDTH)),
            _const_spec((1, A_WIDTH)),
            _const_spec((A_GROUPS, GM_BLOCK, GM_BLOCK)),
            _const_spec((GM_BLOCK, A_GROUPS)),
            _const_spec((A_WIDTH, D_MODEL)),
            _const_spec((D_MODEL, D_MODEL)),
            _const_spec((PLE_DIM, D_MODEL)),
        ],
        out_specs=pl.BlockSpec((tm, D_MODEL), lambda i: (i, 0)),
        out_shape=jax.ShapeDtypeStruct((t, D_MODEL), _F32),
        scratch_shapes=[
            pltpu.VMEM((tm, D_MODEL), _BF16),
            pltpu.VMEM((tm, A_WIDTH), _F32),
            pltpu.VMEM((tm, A_WIDTH), _BF16),
            pltpu.VMEM((tm, A_WIDTH), _BF16),
            pltpu.VMEM((tm, D_MODEL), _F32),
            pltpu.VMEM((tm, D_MODEL), _BF16),
        ],
        compiler_params=pltpu.CompilerParams(
            dimension_semantics=("arbitrary",), vmem_limit_bytes=VMEM_LIMIT),
        name="a_layer",
    )(h, p, norm_g, w_in, ln_g, ln_b, w_s, b_s_t, w_out, gate_w, ple_w)


def _b_pre_kernel(with_kv, h_ref, ng_ref, w_in_ref, *rest):
    if with_kv:
        kvg_ref, w_kv_ref, q_ref, gate_ref, kv_ref, hn_ref = rest
    else:
        q_ref, gate_ref, hn_ref = rest
    h = h_ref[...]
    hn_ref[...] = _rms(h, ng_ref[...]).astype(_BF16)
    hn = hn_ref[...]
    for c in range(D_MODEL // COLS):
        cs = slice(c * COLS, (c + 1) * COLS)
        q_ref[:, cs] = _dot(hn, w_in_ref[:, cs]).astype(_BF16)
        gate_ref[:, cs] = _dot(hn, w_in_ref[:, D_MODEL + c * COLS:D_MODEL + (c + 1) * COLS])
    if with_kv:
        hn_ref[...] = _rms(h, kvg_ref[...]).astype(_BF16)
        hk = hn_ref[...]
        for c in range(2 * D_MODEL // COLS):
            cs = slice(c * COLS, (c + 1) * COLS)
            kv_ref[:, cs] = _dot(hk, w_kv_ref[:, cs]).astype(_BF16)


def _b_pre(h, norm_g, w_in, kv_g=None, w_kv=None):
    t = h.shape[0]
    tm = TM_B
    with_kv = w_kv is not None
    in_specs = [
        pl.BlockSpec((tm, D_MODEL), lambda i: (i, 0)),
        _const_spec((1, D_MODEL)),
        _const_spec((D_MODEL, 2 * D_MODEL)),
    ]
    out_specs = [
        pl.BlockSpec((tm, D_MODEL), lambda i: (i, 0)),
        pl.BlockSpec((tm, D_MODEL), lambda i: (i, 0)),
    ]
    out_shape = [
        jax.ShapeDtypeStruct((t, D_MODEL), _BF16),
        jax.ShapeDtypeStruct((t, D_MODEL), _F32),
    ]
    args = [h, norm_g, w_in]
    if with_kv:
        in_specs += [_const_spec((1, D_MODEL)), _const_spec((D_MODEL, 2 * D_MODEL))]
        out_specs.append(pl.BlockSpec((tm, 2 * D_MODEL), lambda i: (i, 0)))
        out_shape.append(jax.ShapeDtypeStruct((t, 2 * D_MODEL), _BF16))
        args += [kv_g, w_kv]
    return pl.pallas_call(
        functools.partial(_b_pre_kernel, with_kv),
        grid=(t // tm,),
        in_specs=in_specs,
        out_specs=out_specs,
        out_shape=out_shape,
        scratch_shapes=[pltpu.VMEM((tm, D_MODEL), _BF16)],
        compiler_params=pltpu.CompilerParams(
            dimension_semantics=("arbitrary",), vmem_limit_bytes=VMEM_LIMIT),
        name="b_pre_kv" if with_kv else "b_pre",
    )(*args)


def _attn_kernel(q_ref, k_ref, v_ref, o_ref):
    i = pl.program_id(2)
    scale = 1.0 / np.sqrt(B_HEAD_DIM)
    q = q_ref[0]

    row = lax.broadcasted_iota(jnp.int32, (TK, TK), 0)
    col = lax.broadcasted_iota(jnp.int32, (TK, TK), 1)
    upper = jnp.where(row > col, 1.0, 0.0).astype(_BF16)

    def block(j, carry, acc, diagonal):
        start = pl.multiple_of(j * TK, TK)
        kb = k_ref[0, pl.ds(start, TK), :]
        vb = v_ref[0, pl.ds(start, TK), :]
        z = lax.dot_general(q, kb, (((1,), (1,)), ((), ())),
                            preferred_element_type=_F32) * scale
        sp = jnp.log1p(jnp.exp(-jnp.abs(z)))
        log_beta = jnp.minimum(z, 0.0) - sp
        log_1m = jnp.minimum(-z, 0.0) - sp
        if diagonal:
            causal = col < row
            log_1m = jnp.where(causal, log_1m, 0.0)
        hi = log_1m.astype(_BF16)
        lo = (log_1m - hi.astype(_F32)).astype(_BF16)
        both = _dot(jnp.concatenate([hi, lo], axis=0), upper)
        excl = both[:TQ] + both[TQ:]
        a = jnp.exp(log_beta + excl + carry)
        if diagonal:
            a = jnp.where(causal, a, 0.0)
        acc = acc + _dot(a.astype(_BF16), vb)
        carry = carry + (excl[:, 0:1] + log_1m[:, 0:1])
        return carry, acc

    carry0 = jnp.zeros((TQ, 1), _F32)
    acc0 = jnp.zeros((TQ, B_HEAD_DIM), _F32)
    carry, acc = block(i, carry0, acc0, True)

    def body(it, state):
        c, a = state
        return block(i - 1 - it, c, a, False)

    carry, acc = lax.fori_loop(0, i, body, (carry, acc))
    o_ref[0] = acc.astype(o_ref.dtype)


def _attention(q3, kv3):
    b, s, _ = q3.shape
    return pl.pallas_call(
        _attn_kernel,
        grid=(b, B_HEADS, s // TQ),
        in_specs=[
            pl.BlockSpec((1, TQ, B_HEAD_DIM), lambda bb, hh, ii: (bb, ii, hh)),
            pl.BlockSpec((1, s, B_HEAD_DIM), lambda bb, hh, ii: (bb, 0, hh)),
            pl.BlockSpec((1, s, B_HEAD_DIM), lambda bb, hh, ii: (bb, 0, B_HEADS + hh)),
        ],
        out_specs=pl.BlockSpec((1, TQ, B_HEAD_DIM), lambda bb, hh, ii: (bb, ii, hh)),
        out_shape=jax.ShapeDtypeStruct((b, s, D_MODEL), _BF16),
        compiler_params=pltpu.CompilerParams(
            dimension_semantics=("arbitrary", "arbitrary", "arbitrary"),
            vmem_limit_bytes=VMEM_LIMIT),
        name="stick_breaking",
    )(q3, kv3, kv3)


def _b_post_kernel(final, o_ref, gate_ref, h_ref, p_ref, w_out_ref, gate_w_ref,
                   ple_w_ref, *rest):
    if final:
        fg_ref, out_ref, y_ref, h1_ref, h1b_ref = rest
    else:
        out_ref, y_ref, h1_ref, h1b_ref = rest
    y_ref[...] = (o_ref[...].astype(_F32) * _silu(gate_ref[...])).astype(_BF16)
    y = y_ref[...]
    for c in range(D_MODEL // COLS):
        cs = slice(c * COLS, (c + 1) * COLS)
        h1 = h_ref[:, cs] + _dot(y, w_out_ref[:, cs])
        h1_ref[:, cs] = h1
        h1b_ref[:, cs] = h1.astype(_BF16)
    _ple_and_store(h1_ref, h1b_ref, p_ref[...].astype(_BF16), gate_w_ref,
                   ple_w_ref, out_ref, fg_ref[...] if final else None)


def _b_post(o, gate, h, p, w_out, gate_w, ple_w, final_g=None):
    t = h.shape[0]
    tm = TM_B
    final = final_g is not None
    in_specs = [
        pl.BlockSpec((tm, D_MODEL), lambda i: (i, 0)),
        pl.BlockSpec((tm, D_MODEL), lambda i: (i, 0)),
        pl.BlockSpec((tm, D_MODEL), lambda i: (i, 0)),
        pl.BlockSpec((tm, PLE_DIM), lambda i: (i, 0)),
        _const_spec((D_MODEL, D_MODEL)),
        _const_spec((D_MODEL, D_MODEL)),
        _const_spec((PLE_DIM, D_MODEL)),
    ]
    args = [o, gate, h, p, w_out, gate_w, ple_w]
    if final:
        in_specs.append(_const_spec((1, D_MODEL)))
        args.append(final_g)
    return pl.pallas_call(
        functools.partial(_b_post_kernel, final),
        grid=(t // tm,),
        in_specs=in_specs,
        out_specs=pl.BlockSpec((tm, D_MODEL), lambda i: (i, 0)),
        out_shape=jax.ShapeDtypeStruct((t, D_MODEL), _F32),
        scratch_shapes=[
            pltpu.VMEM((tm, D_MODEL), _BF16),
            pltpu.VMEM((tm, D_MODEL), _F32),
            pltpu.VMEM((tm, D_MODEL), _BF16),
        ],
        compiler_params=pltpu.CompilerParams(
            dimension_semantics=("arbitrary",), vmem_limit_bytes=VMEM_LIMIT),
        name="b_post_final" if final else "b_post",
    )(*args)


def kernel(x, p, norm_g, a_w_in, a_ln_g, a_ln_b, a_w_s, a_b_s, a_w_out, kv_norm_g, w_kv, b_w_in, b_w_out, ple_w, ple_gate_w, final_g):
    b, s, d = x.shape
    t = b * s
    bf = lambda w: w.astype(_BF16)
    row = lambda g: g.reshape(1, -1)

    h = x.reshape(t, d)
    p2 = p.reshape(DEPTH, t, PLE_DIM)

    for i in range(N_A):
        h = _a_layer(h, p2[i], row(norm_g[i]), bf(a_w_in[i]), row(a_ln_g[i]),
                     row(a_ln_b[i]), a_w_s[i], a_b_s[i].T, bf(a_w_out[i]),
                     bf(ple_gate_w[i]), bf(ple_w[i]))

    kv3 = None
    for j in range(DEPTH - N_A):
        i = N_A + j
        if kv3 is None:
            q, gate, kv = _b_pre(h, row(norm_g[i]), bf(b_w_in[j]),
                                 row(kv_norm_g), bf(w_kv))
            kv3 = kv.reshape(b, s, 2 * d)
        else:
            q, gate = _b_pre(h, row(norm_g[i]), bf(b_w_in[j]))
        o = _attention(q.reshape(b, s, d), kv3).reshape(t, d)
        h = _b_post(o, gate, h, p2[i], bf(b_w_out[j]), bf(ple_gate_w[i]),
                    bf(ple_w[i]), row(final_g) if i == DEPTH - 1 else None)
    return h.reshape(b, s, d)
```

```python
import functools

import jax
import jax.numpy as jnp
import numpy as np
from jax import lax
from jax.experimental import pallas as pl
from jax.experimental.pallas import tpu as pltpu

D_MODEL = 1024
DEPTH = 4
N_A = DEPTH // 2
CHUNK = 64
GM_BLOCK = 128
A_WIDTH = 2 * D_MODEL
A_GROUPS = 8
A_GROUP_CH = A_WIDTH // A_GROUPS
B_HEADS = 8
B_HEAD_DIM = D_MODEL // B_HEADS
PLE_DIM = 256
EPS = 1e-6

COLS = 256
TM_A = 256
TM_B = 512
TQ = 256
TK = 256
HEADS_PER_STEP = 4
VMEM_LIMIT = 56 * 1024 * 1024

_SQRT_HALF = float(np.sqrt(0.5))
_Q_SCALE = float(1.0 / np.sqrt(B_HEAD_DIM))
_F32 = jnp.float32
_BF16 = jnp.bfloat16


def _rms(x, g):
    ms = jnp.mean(x * x, axis=-1, keepdims=True)
    return x * lax.rsqrt(ms + EPS) * g


def _gelu(x):
    return 0.5 * x * (1.0 + lax.erf(x * _SQRT_HALF))


def _silu(x):
    return x * jax.nn.sigmoid(x)


def _dot(a, b):
    return jnp.dot(a, b, preferred_element_type=_F32)


def _ple_and_store(h1_ref, h1b_ref, pb, gate_w_ref, ple_w_ref, out_ref, final_g):
    h1b = h1b_ref[...]
    for c in range(D_MODEL // COLS):
        cs = slice(c * COLS, (c + 1) * COLS)
        pg = _dot(h1b, gate_w_ref[:, cs])
        pe = _dot(pb, ple_w_ref[:, cs])
        out_ref[:, cs] = h1_ref[:, cs] + jax.nn.sigmoid(pg) * pe
    if final_g is not None:
        out_ref[...] = _rms(out_ref[...], final_g)


def _a_layer_kernel(h_ref, p_ref, ng_ref, w_in_ref, lng_ref, lnb_ref, ws_ref,
                    bst_ref, w_out_ref, gate_w_ref, ple_w_ref, out_ref,
                    hn_ref, v_ref, vn_ref, y_ref, h1_ref, h1b_ref):
    tm = h_ref.shape[0]
    n_grp = tm // GM_BLOCK
    n_chunks = A_WIDTH // COLS

    hn_ref[...] = _rms(h_ref[...], ng_ref[...]).astype(_BF16)
    hn = hn_ref[...]

    s1 = jnp.zeros((tm, 1), _F32)
    for c in range(n_chunks):
        cs = slice(c * COLS, (c + 1) * COLS)
        vc = _gelu(_dot(hn, w_in_ref[:, A_WIDTH + c * COLS:A_WIDTH + (c + 1) * COLS]))
        v_ref[:, cs] = vc
        s1 = s1 + jnp.sum(vc, axis=-1, keepdims=True)
    mu = s1 * (1.0 / A_WIDTH)
    s2 = jnp.zeros((tm, 1), _F32)
    for c in range(n_chunks):
        cs = slice(c * COLS, (c + 1) * COLS)
        d = v_ref[:, cs] - mu
        s2 = s2 + jnp.sum(d * d, axis=-1, keepdims=True)
    rstd = lax.rsqrt(s2 * (1.0 / A_WIDTH) + EPS)
    for c in range(n_chunks):
        cs = slice(c * COLS, (c + 1) * COLS)
        vn = (v_ref[:, cs] - mu) * rstd * lng_ref[:, cs] + lnb_ref[:, cs]
        vn_ref[:, cs] = vn.astype(_BF16)

    t_chunk = lax.broadcasted_iota(jnp.int32, (GM_BLOCK, GM_BLOCK), 0) // CHUNK
    s_chunk = lax.broadcasted_iota(jnp.int32, (GM_BLOCK, GM_BLOCK), 1) // CHUNK
    mask = s_chunk <= t_chunk

    for g in range(A_GROUPS):
        cs = slice(g * A_GROUP_CH, (g + 1) * A_GROUP_CH)
        u = _gelu(_dot(hn, w_in_ref[:, cs]))
        gate = _dot(hn, w_in_ref[:, 2 * A_WIDTH + g * A_GROUP_CH:2 * A_WIDTH + (g + 1) * A_GROUP_CH])
        wm = jnp.where(mask, ws_ref[g], 0.0).astype(_BF16)
        b_col = bst_ref[:, g:g + 1]
        sv = jnp.concatenate(
            [_dot(wm, vn_ref[n * GM_BLOCK:(n + 1) * GM_BLOCK, cs]) + b_col
             for n in range(n_grp)], axis=0)
        y_ref[:, cs] = (u * sv * _silu(gate)).astype(_BF16)

    y = y_ref[...]
    for c in range(D_MODEL // COLS):
        cs = slice(c * COLS, (c + 1) * COLS)
        h1 = h_ref[:, cs] + _dot(y, w_out_ref[:, cs])
        h1_ref[:, cs] = h1
        h1b_ref[:, cs] = h1.astype(_BF16)
    _ple_and_store(h1_ref, h1b_ref, p_ref[...].astype(_BF16), gate_w_ref,
                   ple_w_ref, out_ref, None)


def _const_spec(shape):
    nd = len(shape)
    return pl.BlockSpec(shape, lambda i, _nd=nd: (0,) * _nd,
                        pipeline_mode=pl.Buffered(1))


def _a_layer(h, p, norm_g, w_in, ln_g, ln_b, w_s, b_s_t, w_out, gate_w, ple_w):
    t = h.shape[0]
    tm = TM_A
    return pl.pallas_call(
        _a_layer_kernel,
        grid=(t // tm,),
        in_specs=[
            pl.BlockSpec((tm, D_MODEL), lambda i: (i, 0)),
            pl.BlockSpec((tm, PLE_DIM), lambda i: (i, 0)),
            _const_spec((1, D_MODEL)),
            _const_spec((D_MODEL, 3 * A_WIDTH)),
            _const_spec((1, A_WIDTH)),
            _const_spec((1, A_WIDTH)),
            _const_spec((A_GROUPS, GM_BLOCK, GM_BLOCK)),
            _const_spec((GM_BLOCK, A_GROUPS)),
            _const_spec((A_WIDTH, D_MODEL)),
            _const_spec((D_MODEL, D_MODEL)),
            _const_spec((PLE_DIM, D_MODEL)),
        ],
        out_specs=pl.BlockSpec((tm, D_MODEL), lambda i: (i, 0)),
        out_shape=jax.ShapeDtypeStruct((t, D_MODEL), _F32),
        scratch_shapes=[
            pltpu.VMEM((tm, D_MODEL), _BF16),
            pltpu.VMEM((tm, A_WIDTH), _F32),
            pltpu.VMEM((tm, A_WIDTH), _BF16),
            pltpu.VMEM((tm, A_WIDTH), _BF16),
            pltpu.VMEM((tm, D_MODEL), _F32),
            pltpu.VMEM((tm, D_MODEL), _BF16),
        ],
        compiler_params=pltpu.CompilerParams(
            dimension_semantics=("arbitrary",), vmem_limit_bytes=VMEM_LIMIT),
        name="a_layer",
    )(h, p, norm_g, w_in, ln_g, ln_b, w_s, b_s_t, w_out, gate_w, ple_w)


def _b_pre_kernel(with_kv, h_ref, ng_ref, w_in_ref, *rest):
    if with_kv:
        kvg_ref, w_kv_ref, q_ref, gate_ref, kv_ref, hn_ref = rest
    else:
        q_ref, gate_ref, hn_ref = rest
    h = h_ref[...]
    hn_ref[...] = _rms(h, ng_ref[...]).astype(_BF16)
    hn = hn_ref[...]
    for c in range(D_MODEL // COLS):
        cs = slice(c * COLS, (c + 1) * COLS)
        q_ref[:, cs] = (_dot(hn, w_in_ref[:, cs]) * _Q_SCALE).astype(_BF16)
        gate_ref[:, cs] = _dot(hn, w_in_ref[:, D_MODEL + c * COLS:D_MODEL + (c + 1) * COLS])
    if with_kv:
        hn_ref[...] = _rms(h, kvg_ref[...]).astype(_BF16)
        hk = hn_ref[...]
        for c in range(2 * D_MODEL // COLS):
            cs = slice(c * COLS, (c + 1) * COLS)
            kv_ref[:, cs] = _dot(hk, w_kv_ref[:, cs]).astype(_BF16)


def _b_pre(h, norm_g, w_in, kv_g=None, w_kv=None):
    t = h.shape[0]
    tm = TM_B
    with_kv = w_kv is not None
    in_specs = [
        pl.BlockSpec((tm, D_MODEL), lambda i: (i, 0)),
        _const_spec((1, D_MODEL)),
        _const_spec((D_MODEL, 2 * D_MODEL)),
    ]
    out_specs = [
        pl.BlockSpec((tm, D_MODEL), lambda i: (i, 0)),
        pl.BlockSpec((tm, D_MODEL), lambda i: (i, 0)),
    ]
    out_shape = [
        jax.ShapeDtypeStruct((t, D_MODEL), _BF16),
        jax.ShapeDtypeStruct((t, D_MODEL), _F32),
    ]
    args = [h, norm_g, w_in]
    if with_kv:
        in_specs += [_const_spec((1, D_MODEL)), _const_spec((D_MODEL, 2 * D_MODEL))]
        out_specs.append(pl.BlockSpec((tm, 2 * D_MODEL), lambda i: (i, 0)))
        out_shape.append(jax.ShapeDtypeStruct((t, 2 * D_MODEL), _BF16))
        args += [kv_g, w_kv]
    return pl.pallas_call(
        functools.partial(_b_pre_kernel, with_kv),
        grid=(t // tm,),
        in_specs=in_specs,
        out_specs=out_specs,
        out_shape=out_shape,
        scratch_shapes=[pltpu.VMEM((tm, D_MODEL), _BF16)],
        compiler_params=pltpu.CompilerParams(
            dimension_semantics=("arbitrary",), vmem_limit_bytes=VMEM_LIMIT),
        name="b_pre_kv" if with_kv else "b_pre",
    )(*args)


def _attn_kernel(q_ref, k_ref, v_ref, o_ref):
    i = pl.program_id(2)
    heads = [slice(hh * B_HEAD_DIM, (hh + 1) * B_HEAD_DIM) for hh in range(HEADS_PER_STEP)]
    qs = [q_ref[0, :, hs] for hs in heads]

    row = lax.broadcasted_iota(jnp.int32, (TK, TK), 0)
    col = lax.broadcasted_iota(jnp.int32, (TK, TK), 1)
    upper = jnp.where(row > col, 1.0, 0.0).astype(_BF16)

    causal = col < row

    def blocks(j, state, diagonal):
        start = pl.multiple_of(j * TK, TK)
        n = range(HEADS_PER_STEP)
        zs = [lax.dot_general(qs[h], k_ref[0, pl.ds(start, TK), heads[h]],
                              (((1,), (1,)), ((), ())), preferred_element_type=_F32)
              for h in n]
        log_beta, log_1m = [], []
        for h in n:
            sp = jnp.log(1.0 + jnp.exp(-jnp.abs(zs[h])))
            log_beta.append(jnp.minimum(zs[h], 0.0) - sp)
            l1m = jnp.minimum(-zs[h], 0.0) - sp
            log_1m.append(jnp.where(causal, l1m, 0.0) if diagonal else l1m)
        excl = []
        for h in n:
            hi = log_1m[h].astype(_BF16)
            lo = (log_1m[h] - hi.astype(_F32)).astype(_BF16)
            both = _dot(jnp.concatenate([hi, lo], axis=0), upper)
            excl.append(both[:TQ] + both[TQ:])
        out = []
        for h in n:
            carry, acc = state[2 * h], state[2 * h + 1]
            a = jnp.exp(log_beta[h] + excl[h] + carry)
            if diagonal:
                a = jnp.where(causal, a, 0.0)
            acc = acc + _dot(a.astype(_BF16), v_ref[0, pl.ds(start, TK), heads[h]])
            carry = carry + (excl[h][:, 0:1] + log_1m[h][:, 0:1])
            out.extend((carry, acc))
        return tuple(out)

    state = (jnp.zeros((TQ, 1), _F32), jnp.zeros((TQ, B_HEAD_DIM), _F32)) * HEADS_PER_STEP
    state = blocks(i, state, True)
    state = lax.fori_loop(0, i, lambda it, st: blocks(i - 1 - it, st, False), state)
    for hh, hs in enumerate(heads):
        o_ref[0, :, hs] = state[2 * hh + 1].astype(o_ref.dtype)


def _attention(q3, kv3):
    b, s, _ = q3.shape
    width = HEADS_PER_STEP * B_HEAD_DIM
    n_grp = B_HEADS // HEADS_PER_STEP
    return pl.pallas_call(
        _attn_kernel,
        grid=(b, n_grp, s // TQ),
        in_specs=[
            pl.BlockSpec((1, TQ, width), lambda bb, hh, ii: (bb, ii, hh)),
            pl.BlockSpec((1, s, width), lambda bb, hh, ii: (bb, 0, hh)),
            pl.BlockSpec((1, s, width), lambda bb, hh, ii: (bb, 0, n_grp + hh)),
        ],
        out_specs=pl.BlockSpec((1, TQ, width), lambda bb, hh, ii: (bb, ii, hh)),
        out_shape=jax.ShapeDtypeStruct((b, s, D_MODEL), _BF16),
        compiler_params=pltpu.CompilerParams(
            dimension_semantics=("arbitrary", "arbitrary", "arbitrary"),
            vmem_limit_bytes=VMEM_LIMIT),
        name="stick_breaking",
    )(q3, kv3, kv3)


def _b_post_kernel(final, o_ref, gate_ref, h_ref, p_ref, w_out_ref, gate_w_ref,
                   ple_w_ref, *rest):
    if final:
        fg_ref, out_ref, y_ref, h1_ref, h1b_ref = rest
    else:
        out_ref, y_ref, h1_ref, h1b_ref = rest
    y_ref[...] = (o_ref[...].astype(_F32) * _silu(gate_ref[...])).astype(_BF16)
    y = y_ref[...]
    for c in range(D_MODEL // COLS):
        cs = slice(c * COLS, (c + 1) * COLS)
        h1 = h_ref[:, cs] + _dot(y, w_out_ref[:, cs])
        h1_ref[:, cs] = h1
        h1b_ref[:, cs] = h1.astype(_BF16)
    _ple_and_store(h1_ref, h1b_ref, p_ref[...].astype(_BF16), gate_w_ref,
                   ple_w_ref, out_ref, fg_ref[...] if final else None)


def _b_post(o, gate, h, p, w_out, gate_w, ple_w, final_g=None):
    t = h.shape[0]
    tm = TM_B
    final = final_g is not None
    in_specs = [
        pl.BlockSpec((tm, D_MODEL), lambda i: (i, 0)),
        pl.BlockSpec((tm, D_MODEL), lambda i: (i, 0)),
        pl.BlockSpec((tm, D_MODEL), lambda i: (i, 0)),
        pl.BlockSpec((tm, PLE_DIM), lambda i: (i, 0)),
        _const_spec((D_MODEL, D_MODEL)),
        _const_spec((D_MODEL, D_MODEL)),
        _const_spec((PLE_DIM, D_MODEL)),
    ]
    args = [o, gate, h, p, w_out, gate_w, ple_w]
    if final:
        in_specs.append(_const_spec((1, D_MODEL)))
        args.append(final_g)
    return pl.pallas_call(
        functools.partial(_b_post_kernel, final),
        grid=(t // tm,),
        in_specs=in_specs,
        out_specs=pl.BlockSpec((tm, D_MODEL), lambda i: (i, 0)),
        out_shape=jax.ShapeDtypeStruct((t, D_MODEL), _F32),
        scratch_shapes=[
            pltpu.VMEM((tm, D_MODEL), _BF16),
            pltpu.VMEM((tm, D_MODEL), _F32),
            pltpu.VMEM((tm, D_MODEL), _BF16),
        ],
        compiler_params=pltpu.CompilerParams(
            dimension_semantics=("arbitrary",), vmem_limit_bytes=VMEM_LIMIT),
        name="b_post_final" if final else "b_post",
    )(*args)


def kernel(x, p, norm_g, a_w_in, a_ln_g, a_ln_b, a_w_s, a_b_s, a_w_out, kv_norm_g, w_kv, b_w_in, b_w_out, ple_w, ple_gate_w, final_g):
    b, s, d = x.shape
    t = b * s
    bf = lambda w: w.astype(_BF16)
    row = lambda g: g.reshape(1, -1)

    h = x.reshape(t, d)
    p2 = p.reshape(DEPTH, t, PLE_DIM)

    for i in range(N_A):
        h = _a_layer(h, p2[i], row(norm_g[i]), bf(a_w_in[i]), row(a_ln_g[i]),
                     row(a_ln_b[i]), a_w_s[i], a_b_s[i].T, bf(a_w_out[i]),
                     bf(ple_gate_w[i]), bf(ple_w[i]))

    kv3 = None
    for j in range(DEPTH - N_A):
        i = N_A + j
        if kv3 is None:
            q, gate, kv = _b_pre(h, row(norm_g[i]), bf(b_w_in[j]),
                                 row(kv_norm_g), bf(w_kv))
            kv3 = kv.reshape(b, s, 2 * d)
        else:
            q, gate = _b_pre(h, row(norm_g[i]), bf(b_w_in[j]))
        o = _attention(q.reshape(b, s, d), kv3).reshape(t, d)
        h = _b_post(o, gate, h, p2[i], bf(b_w_out[j]), bf(ple_gate_w[i]),
                    bf(ple_w[i]), row(final_g) if i == DEPTH - 1 else None)
    return h.reshape(b, s, d)
```

```python
import functools

import jax
import jax.numpy as jnp
import numpy as np
from jax import lax
from jax.experimental import pallas as pl
from jax.experimental.pallas import tpu as pltpu

D_MODEL = 1024
DEPTH = 4
N_A = DEPTH // 2
CHUNK = 64
GM_BLOCK = 128
A_WIDTH = 2 * D_MODEL
A_GROUPS = 8
A_GROUP_CH = A_WIDTH // A_GROUPS
B_HEADS = 8
B_HEAD_DIM = D_MODEL // B_HEADS
PLE_DIM = 256
EPS = 1e-6

COLS = 256
TM_A = 256
TM_B = 512
TQ = 256
TK = 256
HEADS_PER_STEP = 8
VMEM_LIMIT = 56 * 1024 * 1024

_SQRT_HALF = float(np.sqrt(0.5))
_Q_SCALE = float(1.0 / np.sqrt(B_HEAD_DIM))
_EXP_ZERO_CUTOFF = -105.0
_F32 = jnp.float32
_BF16 = jnp.bfloat16


def _rms(x, g):
    ms = jnp.mean(x * x, axis=-1, keepdims=True)
    return x * lax.rsqrt(ms + EPS) * g


def _gelu(x):
    return 0.5 * x * (1.0 + lax.erf(x * _SQRT_HALF))


def _silu(x):
    return x * jax.nn.sigmoid(x)


def _dot(a, b):
    return jnp.dot(a, b, preferred_element_type=_F32)


def _ple_and_store(h1_ref, h1b_ref, pb, gate_w_ref, ple_w_ref, out_ref, final_g):
    h1b = h1b_ref[...]
    for c in range(D_MODEL // COLS):
        cs = slice(c * COLS, (c + 1) * COLS)
        pg = _dot(h1b, gate_w_ref[:, cs])
        pe = _dot(pb, ple_w_ref[:, cs])
        out_ref[:, cs] = h1_ref[:, cs] + jax.nn.sigmoid(pg) * pe
    if final_g is not None:
        out_ref[...] = _rms(out_ref[...], final_g)


def _a_layer_kernel(h_ref, p_ref, ng_ref, w_in_ref, lng_ref, lnb_ref, ws_ref,
                    bst_ref, w_out_ref, gate_w_ref, ple_w_ref, out_ref,
                    hn_ref, v_ref, vn_ref, y_ref, h1_ref, h1b_ref):
    tm = h_ref.shape[0]
    n_grp = tm // GM_BLOCK
    n_chunks = A_WIDTH // COLS

    hn_ref[...] = _rms(h_ref[...], ng_ref[...]).astype(_BF16)
    hn = hn_ref[...]

    s1 = jnp.zeros((tm, 1), _F32)
    for c in range(n_chunks):
        cs = slice(c * COLS, (c + 1) * COLS)
        vc = _gelu(_dot(hn, w_in_ref[:, A_WIDTH + c * COLS:A_WIDTH + (c + 1) * COLS]))
        v_ref[:, cs] = vc
        s1 = s1 + jnp.sum(vc, axis=-1, keepdims=True)
    mu = s1 * (1.0 / A_WIDTH)
    s2 = jnp.zeros((tm, 1), _F32)
    for c in range(n_chunks):
        cs = slice(c * COLS, (c + 1) * COLS)
        d = v_ref[:, cs] - mu
        s2 = s2 + jnp.sum(d * d, axis=-1, keepdims=True)
    rstd = lax.rsqrt(s2 * (1.0 / A_WIDTH) + EPS)
    for c in range(n_chunks):
        cs = slice(c * COLS, (c + 1) * COLS)
        vn = (v_ref[:, cs] - mu) * rstd * lng_ref[:, cs] + lnb_ref[:, cs]
        vn_ref[:, cs] = vn.astype(_BF16)

    t_chunk = lax.broadcasted_iota(jnp.int32, (GM_BLOCK, GM_BLOCK), 0) // CHUNK
    s_chunk = lax.broadcasted_iota(jnp.int32, (GM_BLOCK, GM_BLOCK), 1) // CHUNK
    mask = s_chunk <= t_chunk

    for g in range(A_GROUPS):
        cs = slice(g * A_GROUP_CH, (g + 1) * A_GROUP_CH)
        u = _gelu(_dot(hn, w_in_ref[:, cs]))
        gate = _dot(hn, w_in_ref[:, 2 * A_WIDTH + g * A_GROUP_CH:2 * A_WIDTH + (g + 1) * A_GROUP_CH])
        wm = jnp.where(mask, ws_ref[g], 0.0).astype(_BF16)
        b_col = bst_ref[:, g:g + 1]
        sv = jnp.concatenate(
            [_dot(wm, vn_ref[n * GM_BLOCK:(n + 1) * GM_BLOCK, cs]) + b_col
             for n in range(n_grp)], axis=0)
        y_ref[:, cs] = (u * sv * _silu(gate)).astype(_BF16)

    y = y_ref[...]
    for c in range(D_MODEL // COLS):
        cs = slice(c * COLS, (c + 1) * COLS)
        h1 = h_ref[:, cs] + _dot(y, w_out_ref[:, cs])
        h1_ref[:, cs] = h1
        h1b_ref[:, cs] = h1.astype(_BF16)
    _ple_and_store(h1_ref, h1b_ref, p_ref[...].astype(_BF16), gate_w_ref,
                   ple_w_ref, out_ref, None)


def _const_spec(shape):
    nd = len(shape)
    return pl.BlockSpec(shape, lambda i, _nd=nd: (0,) * _nd,
                        pipeline_mode=pl.Buffered(1))


def _a_layer(h, p, norm_g, w_in, ln_g, ln_b, w_s, b_s_t, w_out, gate_w, ple_w):
    t = h.shape[0]
    tm = TM_A
    return pl.pallas_call(
        _a_layer_kernel,
        grid=(t // tm,),
        in_specs=[
            pl.BlockSpec((tm, D_MODEL), lambda i: (i, 0)),
            pl.BlockSpec((tm, PLE_DIM), lambda i: (i, 0)),
            _const_spec((1, D_MODEL)),
            _const_spec((D_MODEL, 3 * A_WIDTH)),
            _const_spec((1, A_WIDTH)),
            _const_spec((1, A_WIDTH)),
            _const_spec((A_GROUPS, GM_BLOCK, GM_BLOCK)),
            _const_spec((GM_BLOCK, A_GROUPS)),
            _const_spec((A_WIDTH, D_MODEL)),
            _const_spec((D_MODEL, D_MODEL)),
            _const_spec((PLE_DIM, D_MODEL)),
        ],
        out_specs=pl.BlockSpec((tm, D_MODEL), lambda i: (i, 0)),
        out_shape=jax.ShapeDtypeStruct((t, D_MODEL), _F32),
        scratch_shapes=[
            pltpu.VMEM((tm, D_MODEL), _BF16),
            pltpu.VMEM((tm, A_WIDTH), _F32),
            pltpu.VMEM((tm, A_WIDTH), _BF16),
            pltpu.VMEM((tm, A_WIDTH), _BF16),
            pltpu.VMEM((tm, D_MODEL), _F32),
            pltpu.VMEM((tm, D_MODEL), _BF16),
        ],
        compiler_params=pltpu.CompilerParams(
            dimension_semantics=("arbitrary",), vmem_limit_bytes=VMEM_LIMIT),
        name="a_layer",
    )(h, p, norm_g, w_in, ln_g, ln_b, w_s, b_s_t, w_out, gate_w, ple_w)


def _b_pre_kernel(with_kv, h_ref, ng_ref, w_in_ref, *rest):
    if with_kv:
        kvg_ref, w_kv_ref, q_ref, gate_ref, kv_ref, hn_ref = rest
    else:
        q_ref, gate_ref, hn_ref = rest
    h = h_ref[...]
    hn_ref[...] = _rms(h, ng_ref[...]).astype(_BF16)
    hn = hn_ref[...]
    for c in range(D_MODEL // COLS):
        cs = slice(c * COLS, (c + 1) * COLS)
        q_ref[:, cs] = (_dot(hn, w_in_ref[:, cs]) * _Q_SCALE).astype(_BF16)
        gate_ref[:, cs] = _dot(hn, w_in_ref[:, D_MODEL + c * COLS:D_MODEL + (c + 1) * COLS])
    if with_kv:
        hn_ref[...] = _rms(h, kvg_ref[...]).astype(_BF16)
        hk = hn_ref[...]
        for c in range(2 * D_MODEL // COLS):
            cs = slice(c * COLS, (c + 1) * COLS)
            kv_ref[:, cs] = _dot(hk, w_kv_ref[:, cs]).astype(_BF16)


def _b_pre(h, norm_g, w_in, kv_g=None, w_kv=None):
    t = h.shape[0]
    tm = TM_B
    with_kv = w_kv is not None
    in_specs = [
        pl.BlockSpec((tm, D_MODEL), lambda i: (i, 0)),
        _const_spec((1, D_MODEL)),
        _const_spec((D_MODEL, 2 * D_MODEL)),
    ]
    out_specs = [
        pl.BlockSpec((tm, D_MODEL), lambda i: (i, 0)),
        pl.BlockSpec((tm, D_MODEL), lambda i: (i, 0)),
    ]
    out_shape = [
        jax.ShapeDtypeStruct((t, D_MODEL), _BF16),
        jax.ShapeDtypeStruct((t, D_MODEL), _F32),
    ]
    args = [h, norm_g, w_in]
    if with_kv:
        in_specs += [_const_spec((1, D_MODEL)), _const_spec((D_MODEL, 2 * D_MODEL))]
        out_specs.append(pl.BlockSpec((tm, 2 * D_MODEL), lambda i: (i, 0)))
        out_shape.append(jax.ShapeDtypeStruct((t, 2 * D_MODEL), _BF16))
        args += [kv_g, w_kv]
    return pl.pallas_call(
        functools.partial(_b_pre_kernel, with_kv),
        grid=(t // tm,),
        in_specs=in_specs,
        out_specs=out_specs,
        out_shape=out_shape,
        scratch_shapes=[pltpu.VMEM((tm, D_MODEL), _BF16)],
        compiler_params=pltpu.CompilerParams(
            dimension_semantics=("arbitrary",), vmem_limit_bytes=VMEM_LIMIT),
        name="b_pre_kv" if with_kv else "b_pre",
    )(*args)


def _attn_kernel(q_ref, k_ref, v_ref, o_ref):
    i = pl.program_id(2)
    heads = [slice(hh * B_HEAD_DIM, (hh + 1) * B_HEAD_DIM) for hh in range(HEADS_PER_STEP)]
    qs = [q_ref[0, :, hs] for hs in heads]

    row = lax.broadcasted_iota(jnp.int32, (TK, TK), 0)
    col = lax.broadcasted_iota(jnp.int32, (TK, TK), 1)
    upper = jnp.where(row > col, 1.0, 0.0).astype(_BF16)

    causal = col < row

    def blocks(j, state, diagonal):
        start = pl.multiple_of(j * TK, TK)
        n = range(HEADS_PER_STEP)
        zs = [lax.dot_general(qs[h], k_ref[0, pl.ds(start, TK), heads[h]],
                              (((1,), (1,)), ((), ())), preferred_element_type=_F32)
              for h in n]
        log_beta, log_1m = [], []
        for h in n:
            sp = jnp.log(1.0 + jnp.exp(-jnp.abs(zs[h])))
            l1m = jnp.minimum(-zs[h], 0.0) - sp
            log_beta.append(l1m + zs[h])
            log_1m.append(jnp.where(causal, l1m, 0.0) if diagonal else l1m)
        excl = [_dot(log_1m[h].astype(_BF16), upper) for h in n]
        out = []
        for h in n:
            carry, acc = state[2 * h], state[2 * h + 1]
            a = jnp.exp(log_beta[h] + excl[h] + carry)
            if diagonal:
                a = jnp.where(causal, a, 0.0)
            acc = acc + _dot(a.astype(_BF16), v_ref[0, pl.ds(start, TK), heads[h]])
            carry = carry + (excl[h][:, 0:1] + log_1m[h][:, 0:1])
            out.extend((carry, acc))
        return tuple(out)

    def live(state):
        top = state[0]
        for h in range(1, HEADS_PER_STEP):
            top = jnp.maximum(top, state[2 * h])
        return (jnp.max(top) > _EXP_ZERO_CUTOFF).astype(jnp.int32)

    state = (jnp.zeros((TQ, 1), _F32), jnp.zeros((TQ, B_HEAD_DIM), _F32)) * HEADS_PER_STEP
    state = blocks(i, state, True)

    def cond(loop):
        j, go = loop[0], loop[1]
        return jnp.logical_and(j >= 0, go > 0)

    def body(loop):
        j, st = loop[0], blocks(loop[0], loop[2], False)
        return j - 1, live(st), st

    state = lax.while_loop(cond, body, (i - 1, live(state), state))[2]
    for hh, hs in enumerate(heads):
        o_ref[0, :, hs] = state[2 * hh + 1].astype(o_ref.dtype)


def _attention(q3, kv3):
    b, s, _ = q3.shape
    width = HEADS_PER_STEP * B_HEAD_DIM
    n_grp = B_HEADS // HEADS_PER_STEP
    return pl.pallas_call(
        _attn_kernel,
        grid=(b, n_grp, s // TQ),
        in_specs=[
            pl.BlockSpec((1, TQ, width), lambda bb, hh, ii: (bb, ii, hh)),
            pl.BlockSpec((1, s, width), lambda bb, hh, ii: (bb, 0, hh)),
            pl.BlockSpec((1, s, width), lambda bb, hh, ii: (bb, 0, n_grp + hh)),
        ],
        out_specs=pl.BlockSpec((1, TQ, width), lambda bb, hh, ii: (bb, ii, hh)),
        out_shape=jax.ShapeDtypeStruct((b, s, D_MODEL), _BF16),
        compiler_params=pltpu.CompilerParams(
            dimension_semantics=("arbitrary", "arbitrary", "arbitrary"),
            vmem_limit_bytes=VMEM_LIMIT),
        name="stick_breaking",
    )(q3, kv3, kv3)


def _b_post_kernel(final, o_ref, gate_ref, h_ref, p_ref, w_out_ref, gate_w_ref,
                   ple_w_ref, *rest):
    if final:
        fg_ref, out_ref, y_ref, h1_ref, h1b_ref = rest
    else:
        out_ref, y_ref, h1_ref, h1b_ref = rest
    y_ref[...] = (o_ref[...].astype(_F32) * _silu(gate_ref[...])).astype(_BF16)
    y = y_ref[...]
    for c in range(D_MODEL // COLS):
        cs = slice(c * COLS, (c + 1) * COLS)
        h1 = h_ref[:, cs] + _dot(y, w_out_ref[:, cs])
        h1_ref[:, cs] = h1
        h1b_ref[:, cs] = h1.astype(_BF16)
    _ple_and_store(h1_ref, h1b_ref, p_ref[...].astype(_BF16), gate_w_ref,
                   ple_w_ref, out_ref, fg_ref[...] if final else None)


def _b_post(o, gate, h, p, w_out, gate_w, ple_w, final_g=None):
    t = h.shape[0]
    tm = TM_B
    final = final_g is not None
    in_specs = [
        pl.BlockSpec((tm, D_MODEL), lambda i: (i, 0)),
        pl.BlockSpec((tm, D_MODEL), lambda i: (i, 0)),
        pl.BlockSpec((tm, D_MODEL), lambda i: (i, 0)),
        pl.BlockSpec((tm, PLE_DIM), lambda i: (i, 0)),
        _const_spec((D_MODEL, D_MODEL)),
        _const_spec((D_MODEL, D_MODEL)),
        _const_spec((PLE_DIM, D_MODEL)),
    ]
    args = [o, gate, h, p, w_out, gate_w, ple_w]
    if final:
        in_specs.append(_const_spec((1, D_MODEL)))
        args.append(final_g)
    return pl.pallas_call(
        functools.partial(_b_post_kernel, final),
        grid=(t // tm,),
        in_specs=in_specs,
        out_specs=pl.BlockSpec((tm, D_MODEL), lambda i: (i, 0)),
        out_shape=jax.ShapeDtypeStruct((t, D_MODEL), _F32),
        scratch_shapes=[
            pltpu.VMEM((tm, D_MODEL), _BF16),
            pltpu.VMEM((tm, D_MODEL), _F32),
            pltpu.VMEM((tm, D_MODEL), _BF16),
        ],
        compiler_params=pltpu.CompilerParams(
            dimension_semantics=("arbitrary",), vmem_limit_bytes=VMEM_LIMIT),
        name="b_post_final" if final else "b_post",
    )(*args)


def kernel(x, p, norm_g, a_w_in, a_ln_g, a_ln_b, a_w_s, a_b_s, a_w_out, kv_norm_g, w_kv, b_w_in, b_w_out, ple_w, ple_gate_w, final_g):
    b, s, d = x.shape
    t = b * s
    bf = lambda w: w.astype(_BF16)
    row = lambda g: g.reshape(1, -1)

    h = x.reshape(t, d)
    p2 = p.reshape(DEPTH, t, PLE_DIM)

    for i in range(N_A):
        h = _a_layer(h, p2[i], row(norm_g[i]), bf(a_w_in[i]), row(a_ln_g[i]),
                     row(a_ln_b[i]), a_w_s[i], a_b_s[i].T, bf(a_w_out[i]),
                     bf(ple_gate_w[i]), bf(ple_w[i]))

    kv3 = None
    for j in range(DEPTH - N_A):
        i = N_A + j
        if kv3 is None:
            q, gate, kv = _b_pre(h, row(norm_g[i]), bf(b_w_in[j]),
                                 row(kv_norm_g), bf(w_kv))
            kv3 = kv.reshape(b, s, 2 * d)
        else:
            q, gate = _b_pre(h, row(norm_g[i]), bf(b_w_in[j]))
        o = _attention(q.reshape(b, s, d), kv3).reshape(t, d)
        h = _b_post(o, gate, h, p2[i], bf(b_w_out[j]), bf(ple_gate_w[i]),
                    bf(ple_w[i]), row(final_g) if i == DEPTH - 1 else None)
    return h.reshape(b, s, d)
```

```python
import jax
import jax.numpy as jnp
import numpy as np
from jax import lax
from jax.experimental import pallas as pl
from jax.experimental.pallas import tpu as pltpu

D_MODEL = 1024
DEPTH = 4
N_A = DEPTH // 2
CHUNK = 64
GM_BLOCK = 128
A_WIDTH = 2 * D_MODEL
A_GROUPS = 8
A_GROUP_CH = A_WIDTH // A_GROUPS
B_HEADS = 8
B_HEAD_DIM = D_MODEL // B_HEADS
PLE_DIM = 256
EPS = 1e-6

COLS = 256
TM_A = 512
TM_B = 512
TQ = 256
TK = 256
VMEM_LIMIT = 60 * 1024 * 1024

_SQRT_HALF = float(np.sqrt(0.5))
_Q_SCALE = float(1.0 / np.sqrt(B_HEAD_DIM))
_EXP_ZERO_CUTOFF = -105.0
_HUGE_CARRY = 1e30
_F32 = jnp.float32
_BF16 = jnp.bfloat16


def _rms(x, g):
    ms = jnp.mean(x * x, axis=-1, keepdims=True)
    return x * lax.rsqrt(ms + EPS) * g


def _gelu(x):
    return 0.5 * x * (1.0 + lax.erf(x * _SQRT_HALF))


def _silu(x):
    return x * jax.nn.sigmoid(x)


def _dot(a, b):
    return jnp.dot(a, b, preferred_element_type=_F32)


def _ple_and_store(h1_ref, h1b_ref, pb, gate_w_ref, ple_w_ref, out_ref, final_g):
    h1b = h1b_ref[...]
    for c in range(D_MODEL // COLS):
        cs = slice(c * COLS, (c + 1) * COLS)
        pg = _dot(h1b, gate_w_ref[:, cs])
        pe = _dot(pb, ple_w_ref[:, cs])
        out_ref[:, cs] = h1_ref[:, cs] + jax.nn.sigmoid(pg) * pe
    if final_g is not None:
        out_ref[...] = _rms(out_ref[...], final_g)


def _a_layer_kernel(h_ref, p_ref, ng_ref, w_in_ref, lng_ref, lnb_ref, ws_ref,
                    bst_ref, w_out_ref, gate_w_ref, ple_w_ref, out_ref,
                    hn_ref, v_ref, vn_ref, y_ref, h1_ref, h1b_ref):
    tm = h_ref.shape[0]
    n_grp = tm // GM_BLOCK
    n_chunks = A_WIDTH // COLS

    hn_ref[...] = _rms(h_ref[...], ng_ref[...]).astype(_BF16)
    hn = hn_ref[...]

    s1 = jnp.zeros((tm, 1), _F32)
    for c in range(n_chunks):
        cs = slice(c * COLS, (c + 1) * COLS)
        vc = _gelu(_dot(hn, w_in_ref[:, A_WIDTH + c * COLS:A_WIDTH + (c + 1) * COLS]))
        v_ref[:, cs] = vc
        s1 = s1 + jnp.sum(vc, axis=-1, keepdims=True)
    mu = s1 * (1.0 / A_WIDTH)
    s2 = jnp.zeros((tm, 1), _F32)
    for c in range(n_chunks):
        cs = slice(c * COLS, (c + 1) * COLS)
        d = v_ref[:, cs] - mu
        s2 = s2 + jnp.sum(d * d, axis=-1, keepdims=True)
    rstd = lax.rsqrt(s2 * (1.0 / A_WIDTH) + EPS)
    for c in range(n_chunks):
        cs = slice(c * COLS, (c + 1) * COLS)
        vn = (v_ref[:, cs] - mu) * rstd * lng_ref[:, cs] + lnb_ref[:, cs]
        vn_ref[:, cs] = vn.astype(_BF16)

    t_chunk = lax.broadcasted_iota(jnp.int32, (GM_BLOCK, GM_BLOCK), 0) // CHUNK
    s_chunk = lax.broadcasted_iota(jnp.int32, (GM_BLOCK, GM_BLOCK), 1) // CHUNK
    mask = s_chunk <= t_chunk

    for g in range(A_GROUPS):
        cs = slice(g * A_GROUP_CH, (g + 1) * A_GROUP_CH)
        u = _gelu(_dot(hn, w_in_ref[:, cs]))
        gate = _dot(hn, w_in_ref[:, 2 * A_WIDTH + g * A_GROUP_CH:2 * A_WIDTH + (g + 1) * A_GROUP_CH])
        wm = jnp.where(mask, ws_ref[g], 0.0).astype(_BF16)
        b_col = bst_ref[:, g:g + 1]
        sv = jnp.concatenate(
            [_dot(wm, vn_ref[n * GM_BLOCK:(n + 1) * GM_BLOCK, cs]) + b_col
             for n in range(n_grp)], axis=0)
        y_ref[:, cs] = (u * sv * _silu(gate)).astype(_BF16)

    y = y_ref[...]
    for c in range(D_MODEL // COLS):
        cs = slice(c * COLS, (c + 1) * COLS)
        h1 = h_ref[:, cs] + _dot(y, w_out_ref[:, cs])
        h1_ref[:, cs] = h1
        h1b_ref[:, cs] = h1.astype(_BF16)
    _ple_and_store(h1_ref, h1b_ref, p_ref[...].astype(_BF16), gate_w_ref,
                   ple_w_ref, out_ref, None)


def _const_spec(shape):
    nd = len(shape)
    return pl.BlockSpec(shape, lambda i, _nd=nd: (0,) * _nd,
                        pipeline_mode=pl.Buffered(1))


def _a_layer(h, p2, layer, norm_g, w_in, ln_g, ln_b, w_s, b_s_t, w_out, gate_w, ple_w):
    t = h.shape[0]
    tm = TM_A
    return pl.pallas_call(
        _a_layer_kernel,
        grid=(t // tm,),
        in_specs=[
            pl.BlockSpec((tm, D_MODEL), lambda i: (i, 0)),
            pl.BlockSpec((None, tm, PLE_DIM), lambda i: (layer, i, 0)),
            _const_spec((1, D_MODEL)),
            _const_spec((D_MODEL, 3 * A_WIDTH)),
            _const_spec((1, A_WIDTH)),
            _const_spec((1, A_WIDTH)),
            _const_spec((A_GROUPS, GM_BLOCK, GM_BLOCK)),
            _const_spec((GM_BLOCK, A_GROUPS)),
            _const_spec((A_WIDTH, D_MODEL)),
            _const_spec((D_MODEL, D_MODEL)),
            _const_spec((PLE_DIM, D_MODEL)),
        ],
        out_specs=pl.BlockSpec((tm, D_MODEL), lambda i: (i, 0)),
        out_shape=jax.ShapeDtypeStruct((t, D_MODEL), _F32),
        scratch_shapes=[
            pltpu.VMEM((tm, D_MODEL), _BF16),
            pltpu.VMEM((tm, A_WIDTH), _F32),
            pltpu.VMEM((tm, A_WIDTH), _BF16),
            pltpu.VMEM((tm, A_WIDTH), _BF16),
            pltpu.VMEM((tm, D_MODEL), _F32),
            pltpu.VMEM((tm, D_MODEL), _BF16),
        ],
        compiler_params=pltpu.CompilerParams(
            dimension_semantics=("arbitrary",), vmem_limit_bytes=VMEM_LIMIT),
        name="a_layer",
    )(h, p2, norm_g, w_in, ln_g, ln_b, w_s, b_s_t, w_out, gate_w, ple_w)


def _kv_kernel(h_ref, kvg_ref, w_kv_ref, kv_ref, hn_ref):
    hn_ref[...] = _rms(h_ref[...], kvg_ref[...]).astype(_BF16)
    hk = hn_ref[...]
    for c in range(2 * D_MODEL // COLS):
        cs = slice(c * COLS, (c + 1) * COLS)
        kv_ref[:, cs] = _dot(hk, w_kv_ref[:, cs]).astype(_BF16)


def _kv_proj(h, kv_g, w_kv):
    t = h.shape[0]
    tm = TM_B
    return pl.pallas_call(
        _kv_kernel,
        grid=(t // tm,),
        in_specs=[
            pl.BlockSpec((tm, D_MODEL), lambda i: (i, 0)),
            _const_spec((1, D_MODEL)),
            _const_spec((D_MODEL, 2 * D_MODEL)),
        ],
        out_specs=pl.BlockSpec((tm, 2 * D_MODEL), lambda i: (i, 0)),
        out_shape=jax.ShapeDtypeStruct((t, 2 * D_MODEL), _BF16),
        scratch_shapes=[pltpu.VMEM((tm, D_MODEL), _BF16)],
        compiler_params=pltpu.CompilerParams(
            dimension_semantics=("arbitrary",), vmem_limit_bytes=VMEM_LIMIT),
        name="kv_proj",
    )(h, kv_g, w_kv)


_HEAD_SLICES = [slice(hh * B_HEAD_DIM, (hh + 1) * B_HEAD_DIM) for hh in range(B_HEADS)]


def _stick_breaking(q_ref, k_ref, v_ref, i):
    heads = _HEAD_SLICES
    n = range(B_HEADS)
    qs = [q_ref[:, hs] for hs in heads]

    half = TK // 2
    row = lax.broadcasted_iota(jnp.int32, (TK, TK), 0)
    col = lax.broadcasted_iota(jnp.int32, (TK, TK), 1)
    upper = {TK: jnp.where(row > col, 1.0, 0.0).astype(_BF16)}
    upper[half] = upper[TK][:half, :half]
    causal_top = (col < row)[:half, :half]
    causal_bottom = (col < row)[half:, :]

    def tiles(work):
        zs = [lax.dot_general(q, k_ref[pl.ds(k0, kn), heads[h]],
                              (((1,), (1,)), ((), ())), preferred_element_type=_F32)
              for q, h, k0, kn, _, _ in work]
        nls = []
        for z, (_, _, _, _, mask, _) in zip(zs, work):
            nl = jnp.maximum(z, 0.0) + jnp.log(1.0 + jnp.exp(-jnp.abs(z)))
            nls.append(nl if mask is None else jnp.where(mask, nl, 0.0))
        excl = [_dot(nl.astype(_BF16), upper[item[3]]) for nl, item in zip(nls, work)]
        out = []
        for z, nl, ex, (_, h, k0, kn, mask, carry) in zip(zs, nls, excl, work):
            below = nl + ex if carry is None else nl + ex + carry
            a = jnp.exp(z - below)
            if mask is not None:
                a = jnp.where(mask, a, 0.0)
            pv = _dot(a.astype(_BF16), v_ref[pl.ds(k0, kn), heads[h]])
            out.append((ex[:, 0:1] + nl[:, 0:1], pv))
        return out

    def diagonal_block(j):
        start = pl.multiple_of(j * TK, TK)
        work = []
        for h in n:
            work.append((qs[h][:half], h, start, half, causal_top, None))
            work.append((qs[h][half:], h, start, TK, causal_bottom, None))
        res = tiles(work)
        state = []
        for h in n:
            (tot_t, pv_t), (tot_b, pv_b) = res[2 * h], res[2 * h + 1]
            state.append(jnp.concatenate([tot_t, tot_b], axis=0))
            state.append(jnp.concatenate([pv_t, pv_b], axis=0))
        return tuple(state)

    def full_block(j, state):
        start = pl.multiple_of(j * TK, TK)
        res = tiles([(qs[h], h, start, TK, None, state[2 * h]) for h in n])
        out = []
        for h in n:
            out.extend((state[2 * h] + res[h][0], state[2 * h + 1] + res[h][1]))
        return tuple(out)

    def live(state):
        low = state[0]
        for h in range(1, B_HEADS):
            low = jnp.minimum(low, state[2 * h])
        return (jnp.min(low) < -_EXP_ZERO_CUTOFF).astype(jnp.int32)

    state = diagonal_block(i)
    no_prev = jnp.where(i > 0, 0.0, _HUGE_CARRY)
    state = tuple(x + no_prev if k % 2 == 0 else x for k, x in enumerate(state))
    state = full_block(jnp.maximum(i - 1, 0), state)

    def cond(loop):
        j, go = loop[0], loop[1]
        return jnp.logical_and(j >= 0, go > 0)

    def body(loop):
        j, st = loop[0], full_block(loop[0], loop[2])
        return j - 1, live(st), st

    state = lax.while_loop(cond, body, (i - 2, live(state), state))[2]
    return [state[2 * h + 1] for h in n]


def _b_layers_kernel(h_ref, pa_ref, pb_ref, k_ref, v_ref, ng_ref, w_in_ref, w_out_ref,
                     gate_w_ref, ple_w_ref, fg_ref, out_ref,
                     hn_ref, q_ref, sg_ref, y_ref, h1_ref, h1b_ref, hmid_ref):
    i = pl.program_id(1)
    n_b = DEPTH - N_A
    p_refs = (pa_ref, pb_ref)
    for j in range(n_b):
        h_in = h_ref if j == 0 else hmid_ref
        h_out = out_ref if j == n_b - 1 else hmid_ref
        hn_ref[...] = _rms(h_in[...], ng_ref[j]).astype(_BF16)
        hn = hn_ref[...]
        for c in range(D_MODEL // COLS):
            cs = slice(c * COLS, (c + 1) * COLS)
            q_ref[:, cs] = (_dot(hn, w_in_ref[j, :, cs]) * _Q_SCALE).astype(_BF16)
            gate = _dot(hn, w_in_ref[j, :, D_MODEL + c * COLS:D_MODEL + (c + 1) * COLS])
            sg_ref[:, cs] = _silu(gate).astype(_BF16)
        o = _stick_breaking(q_ref, k_ref, v_ref, i)
        for hh, hs in enumerate(_HEAD_SLICES):
            y_ref[:, hs] = o[hh].astype(_BF16) * sg_ref[:, hs]
        y = y_ref[...]
        for c in range(D_MODEL // COLS):
            cs = slice(c * COLS, (c + 1) * COLS)
            h1 = h_in[:, cs] + _dot(y, w_out_ref[j, :, cs])
            h1_ref[:, cs] = h1
            h1b_ref[:, cs] = h1.astype(_BF16)
        _ple_and_store(h1_ref, h1b_ref, p_refs[j][...].astype(_BF16), gate_w_ref.at[j],
                       ple_w_ref.at[j], h_out, fg_ref[...] if j == n_b - 1 else None)


def _b_layers(h3, p2, kv3, norm_g, w_in, w_out, gate_w, ple_w, final_g):
    b, s, d = h3.shape
    nq = s // TQ
    n_b = DEPTH - N_A

    def const(shape):
        nd = len(shape)
        return pl.BlockSpec(shape, lambda bb, ii, _nd=nd: (0,) * _nd,
                            pipeline_mode=pl.Buffered(1))

    tile = lambda: pl.BlockSpec((None, TQ, d), lambda bb, ii: (bb, ii, 0))
    p_spec = lambda layer: pl.BlockSpec(
        (None, TQ, PLE_DIM), lambda bb, ii, _l=layer: (_l, bb * nq + ii, 0))
    return pl.pallas_call(
        _b_layers_kernel,
        grid=(b, nq),
        in_specs=[
            tile(),
            p_spec(N_A), p_spec(N_A + 1),
            pl.BlockSpec((None, s, d), lambda bb, ii: (bb, 0, 0)),
            pl.BlockSpec((None, s, d), lambda bb, ii: (bb, 0, 1)),
            const((n_b, 1, d)),
            const((n_b, d, 2 * d)),
            const((n_b, d, d)),
            const((n_b, d, d)),
            const((n_b, PLE_DIM, d)),
            const((1, d)),
        ],
        out_specs=tile(),
        out_shape=jax.ShapeDtypeStruct((b, s, d), _F32),
        scratch_shapes=[
            pltpu.VMEM((TQ, d), _BF16),
            pltpu.VMEM((TQ, d), _BF16),
            pltpu.VMEM((TQ, d), _BF16),
            pltpu.VMEM((TQ, d), _BF16),
            pltpu.VMEM((TQ, d), _F32),
            pltpu.VMEM((TQ, d), _BF16),
            pltpu.VMEM((TQ, d), _F32),
        ],
        compiler_params=pltpu.CompilerParams(
            dimension_semantics=("arbitrary", "arbitrary"),
            vmem_limit_bytes=VMEM_LIMIT),
        name="b_layers",
    )(h3, p2, p2, kv3, kv3, norm_g, w_in, w_out, gate_w, ple_w, final_g)


def kernel(x, p, norm_g, a_w_in, a_ln_g, a_ln_b, a_w_s, a_b_s, a_w_out, kv_norm_g, w_kv, b_w_in, b_w_out, ple_w, ple_gate_w, final_g):
    b, s, d = x.shape
    t = b * s
    bf = lambda w: w.astype(_BF16)
    row = lambda g: g.reshape(1, -1)

    h = x.reshape(t, d)
    p2 = p.reshape(DEPTH, t, PLE_DIM)

    for i in range(N_A):
        h = _a_layer(h, p2, i, row(norm_g[i]), bf(a_w_in[i]), row(a_ln_g[i]),
                     row(a_ln_b[i]), a_w_s[i], a_b_s[i].T, bf(a_w_out[i]),
                     bf(ple_gate_w[i]), bf(ple_w[i]))

    kv3 = _kv_proj(h, row(kv_norm_g), bf(w_kv)).reshape(b, s, 2 * d)
    return _b_layers(h.reshape(b, s, d), p2, kv3, norm_g[N_A:].reshape(-1, 1, d),
                     bf(b_w_in), bf(b_w_out), bf(ple_gate_w[N_A:]), bf(ple_w[N_A:]),
                     row(final_g))
```

```python
import jax
import jax.numpy as jnp
import numpy as np
from jax import lax
from jax.experimental import pallas as pl
from jax.experimental.pallas import tpu as pltpu

D_MODEL = 1024
DEPTH = 4
N_A = DEPTH // 2
CHUNK = 64
GM_BLOCK = 128
A_WIDTH = 2 * D_MODEL
A_GROUPS = 8
A_GROUP_CH = A_WIDTH // A_GROUPS
B_HEADS = 8
B_HEAD_DIM = D_MODEL // B_HEADS
PLE_DIM = 256
EPS = 1e-6

COLS = 256
TM_A = 512
TQ = 256
TK = 256
VMEM_LIMIT = 60 * 1024 * 1024

_SQRT_HALF = float(np.sqrt(0.5))
_Q_SCALE = float(1.0 / np.sqrt(B_HEAD_DIM))
_EXP_ZERO_CUTOFF = -105.0
_HUGE_CARRY = 1e30
_F32 = jnp.float32
_BF16 = jnp.bfloat16


def _rms(x, g):
    ms = jnp.mean(x * x, axis=-1, keepdims=True)
    return x * lax.rsqrt(ms + EPS) * g


def _gelu(x):
    return 0.5 * x * (1.0 + lax.erf(x * _SQRT_HALF))


def _silu(x):
    return x * jax.nn.sigmoid(x)


def _dot(a, b):
    return jnp.dot(a, b, preferred_element_type=_F32)


def _ple_and_store(h1_ref, h1b_ref, pb, gate_w_ref, ple_w_ref, out_ref, final_g):
    h1b = h1b_ref[...]
    for c in range(D_MODEL // COLS):
        cs = slice(c * COLS, (c + 1) * COLS)
        pg = _dot(h1b, gate_w_ref[:, cs])
        pe = _dot(pb, ple_w_ref[:, cs])
        out_ref[:, cs] = h1_ref[:, cs] + jax.nn.sigmoid(pg) * pe
    if final_g is not None:
        out_ref[...] = _rms(out_ref[...], final_g)


def _a_layer_kernel(h_ref, p_ref, ng_ref, w_in_ref, lng_ref, lnb_ref, ws_ref,
                    bst_ref, w_out_ref, gate_w_ref, ple_w_ref, out_ref,
                    hn_ref, v_ref, vn_ref, y_ref, h1_ref, h1b_ref):
    tm = h_ref.shape[0]
    n_grp = tm // GM_BLOCK
    n_chunks = A_WIDTH // COLS

    hn_ref[...] = _rms(h_ref[...], ng_ref[...]).astype(_BF16)
    hn = hn_ref[...]

    s1 = jnp.zeros((tm, 1), _F32)
    for c in range(n_chunks):
        cs = slice(c * COLS, (c + 1) * COLS)
        vc = _gelu(_dot(hn, w_in_ref[:, A_WIDTH + c * COLS:A_WIDTH + (c + 1) * COLS]))
        v_ref[:, cs] = vc
        s1 = s1 + jnp.sum(vc, axis=-1, keepdims=True)
    mu = s1 * (1.0 / A_WIDTH)
    s2 = jnp.zeros((tm, 1), _F32)
    for c in range(n_chunks):
        cs = slice(c * COLS, (c + 1) * COLS)
        d = v_ref[:, cs] - mu
        s2 = s2 + jnp.sum(d * d, axis=-1, keepdims=True)
    rstd = lax.rsqrt(s2 * (1.0 / A_WIDTH) + EPS)
    for c in range(n_chunks):
        cs = slice(c * COLS, (c + 1) * COLS)
        vn = (v_ref[:, cs] - mu) * rstd * lng_ref[:, cs] + lnb_ref[:, cs]
        vn_ref[:, cs] = vn.astype(_BF16)

    t_chunk = lax.broadcasted_iota(jnp.int32, (GM_BLOCK, GM_BLOCK), 0) // CHUNK
    s_chunk = lax.broadcasted_iota(jnp.int32, (GM_BLOCK, GM_BLOCK), 1) // CHUNK
    mask = s_chunk <= t_chunk

    for g in range(A_GROUPS):
        cs = slice(g * A_GROUP_CH, (g + 1) * A_GROUP_CH)
        u = _gelu(_dot(hn, w_in_ref[:, cs]))
        gate = _dot(hn, w_in_ref[:, 2 * A_WIDTH + g * A_GROUP_CH:2 * A_WIDTH + (g + 1) * A_GROUP_CH])
        wm = jnp.where(mask, ws_ref[g], 0.0).astype(_BF16)
        b_col = bst_ref[:, g:g + 1]
        sv = jnp.concatenate(
            [_dot(wm, vn_ref[n * GM_BLOCK:(n + 1) * GM_BLOCK, cs]) + b_col
             for n in range(n_grp)], axis=0)
        y_ref[:, cs] = (u * sv * _silu(gate)).astype(_BF16)

    y = y_ref[...]
    for c in range(D_MODEL // COLS):
        cs = slice(c * COLS, (c + 1) * COLS)
        h1 = h_ref[:, cs] + _dot(y, w_out_ref[:, cs])
        h1_ref[:, cs] = h1
        h1b_ref[:, cs] = h1.astype(_BF16)
    _ple_and_store(h1_ref, h1b_ref, p_ref[...].astype(_BF16), gate_w_ref,
                   ple_w_ref, out_ref, None)


def _const_spec(shape):
    nd = len(shape)
    return pl.BlockSpec(shape, lambda i, _nd=nd: (0,) * _nd,
                        pipeline_mode=pl.Buffered(1))


def _a_layer(h, p2, layer, norm_g, w_in, ln_g, ln_b, w_s, b_s_t, w_out, gate_w, ple_w):
    t = h.shape[0]
    tm = TM_A
    return pl.pallas_call(
        _a_layer_kernel,
        grid=(t // tm,),
        in_specs=[
            pl.BlockSpec((tm, D_MODEL), lambda i: (i, 0)),
            pl.BlockSpec((None, tm, PLE_DIM), lambda i: (layer, i, 0)),
            _const_spec((1, D_MODEL)),
            _const_spec((D_MODEL, 3 * A_WIDTH)),
            _const_spec((1, A_WIDTH)),
            _const_spec((1, A_WIDTH)),
            _const_spec((A_GROUPS, GM_BLOCK, GM_BLOCK)),
            _const_spec((GM_BLOCK, A_GROUPS)),
            _const_spec((A_WIDTH, D_MODEL)),
            _const_spec((D_MODEL, D_MODEL)),
            _const_spec((PLE_DIM, D_MODEL)),
        ],
        out_specs=pl.BlockSpec((tm, D_MODEL), lambda i: (i, 0)),
        out_shape=jax.ShapeDtypeStruct((t, D_MODEL), _F32),
        scratch_shapes=[
            pltpu.VMEM((tm, D_MODEL), _BF16),
            pltpu.VMEM((tm, A_WIDTH), _F32),
            pltpu.VMEM((tm, A_WIDTH), _BF16),
            pltpu.VMEM((tm, A_WIDTH), _BF16),
            pltpu.VMEM((tm, D_MODEL), _F32),
            pltpu.VMEM((tm, D_MODEL), _BF16),
        ],
        compiler_params=pltpu.CompilerParams(
            dimension_semantics=("arbitrary",), vmem_limit_bytes=VMEM_LIMIT),
        name="a_layer",
    )(h, p2, norm_g, w_in, ln_g, ln_b, w_s, b_s_t, w_out, gate_w, ple_w)


_HEAD_SLICES = [slice(hh * B_HEAD_DIM, (hh + 1) * B_HEAD_DIM) for hh in range(B_HEADS)]
_COL_SLICES = [slice(c * COLS, (c + 1) * COLS) for c in range(D_MODEL // COLS)]
TILES = 2


def _interleave(primary, filler):
    done = 0
    for k, step in enumerate(primary):
        step()
        upto = (k + 1) * len(filler) // len(primary)
        for f in filler[done:upto]:
            f()
        done = upto


def _b_layers_kernel(h_ref, pa_ref, pb_ref, ng_ref, w_in_ref, w_out_ref, gate_w_ref,
                     ple_w_ref, fg_ref, kvg_ref, w_kv_ref, out_ref,
                     k_ref, v_ref, hn_ref, q_ref, sg_ref):
    step = pl.program_id(1)
    n_b = DEPTH - N_A
    p_refs = (pa_ref, pb_ref)
    heads = _HEAD_SLICES
    rows = [slice(t * TQ, (t + 1) * TQ) for t in range(TILES)]
    blocks = [step * TILES + t for t in range(TILES)]

    half = TK // 2
    row = lax.broadcasted_iota(jnp.int32, (TK, TK), 0)
    col = lax.broadcasted_iota(jnp.int32, (TK, TK), 1)
    upper = {TK: jnp.where(row > col, 1.0, 0.0).astype(_BF16)}
    upper[half] = upper[TK][:half, :half]
    causal_top = (col < row)[:half, :half]
    causal_bottom = (col < row)[half:, :]

    def sweep_steps(items, results):
        zs, nls, excl = {}, {}, {}

        def logits(k):
            q, h, k0, kn = items[k][:4]
            zs[k] = lax.dot_general(q(), k_ref[pl.ds(k0, kn), heads[h]],
                                    (((1,), (1,)), ((), ())), preferred_element_type=_F32)

        def softplus(k):
            z, mask = zs[k], items[k][4]
            nl = jnp.maximum(z, 0.0) + jnp.log(1.0 + jnp.exp(-jnp.abs(z)))
            nls[k] = nl if mask is None else jnp.where(mask, nl, 0.0)

        def suffix(k):
            excl[k] = _dot(nls[k].astype(_BF16), upper[items[k][3]])

        def weigh(k):
            _, h, k0, kn, mask, carry = items[k]
            below = nls[k] + excl[k] if carry is None else nls[k] + excl[k] + carry()
            a = jnp.exp(zs.pop(k) - below)
            if mask is not None:
                a = jnp.where(mask, a, 0.0)
            pv = _dot(a.astype(_BF16), v_ref[pl.ds(k0, kn), heads[h]])
            results[k] = (excl.pop(k)[:, 0:1] + nls.pop(k)[:, 0:1], pv)

        order = range(len(items))
        return [lambda k=k, f=f: f(k) for f in (logits, softplus, suffix, weigh) for k in order]

    def window_steps(t, holder):
        blk = blocks[t]
        start = pl.multiple_of(blk * TK, TK)
        prev = pl.multiple_of(jnp.maximum(blk - 1, 0) * TK, TK)
        no_prev = jnp.where(blk > 0, 0.0, _HUGE_CARRY)
        d_items, d_results, diag = [], {}, {}
        for h, hs in enumerate(heads):
            d_items.append((lambda hs=hs: q_ref[t, :half, hs], h, start, half, causal_top, None))
            d_items.append((lambda hs=hs: q_ref[t, half:, hs], h, start, TK, causal_bottom, None))

        def diag_carry(h):
            if h not in diag:
                diag[h] = jnp.concatenate([d_results[2 * h][0], d_results[2 * h + 1][0]], axis=0)
            return diag[h]

        p_items = [(lambda hs=hs: q_ref[t, :, hs], h, prev, TK, None,
                    lambda h=h: diag_carry(h) + no_prev) for h, hs in enumerate(heads)]
        p_results = {}
        d_steps = sweep_steps(d_items, d_results)
        p_steps = sweep_steps(p_items, p_results)
        nd, npv = len(d_items), len(p_items)
        ordered = []
        for stage in range(3):
            ordered += d_steps[stage * nd:(stage + 1) * nd] + p_steps[stage * npv:(stage + 1) * npv]
        ordered += d_steps[3 * nd:] + p_steps[3 * npv:]

        def finish():
            state = []
            for h in range(B_HEADS):
                pv = jnp.concatenate([d_results[2 * h][1], d_results[2 * h + 1][1]], axis=0)
                state.append(diag_carry(h) + no_prev + p_results[h][0])
                state.append(pv + p_results[h][1])
            holder["state"] = tuple(state)

        return ordered + [finish]

    def live(state):
        low = state[0]
        for h in range(1, B_HEADS):
            low = jnp.minimum(low, state[2 * h])
        return (jnp.min(low) < -_EXP_ZERO_CUTOFF).astype(jnp.int32)

    def finish_sweep(t, holder):
        def full_block(j, state):
            k0 = pl.multiple_of(j * TK, TK)
            res = {}
            items = [(lambda hs=hs: q_ref[t, :, hs], h, k0, TK, None,
                      lambda h=h: state[2 * h]) for h, hs in enumerate(heads)]
            for f in sweep_steps(items, res):
                f()
            out = []
            for h in range(B_HEADS):
                out.extend((state[2 * h] + res[h][0], state[2 * h + 1] + res[h][1]))
            return tuple(out)

        def cond(loop):
            return jnp.logical_and(loop[0] >= 0, loop[1] > 0)

        def body(loop):
            st = full_block(loop[0], loop[2])
            return loop[0] - 1, live(st), st

        st = holder["state"]
        holder["state"] = lax.while_loop(cond, body, (blocks[t] - 2, live(st), st))[2]

    def kv_steps(t):
        r0 = pl.multiple_of(blocks[t] * TQ, TQ)

        def norm():
            hn_ref[t] = _rms(h_ref[rows[t], :], kvg_ref[...]).astype(_BF16)

        def chunk(c):
            kv = _dot(hn_ref[t], w_kv_ref[:, c * COLS:(c + 1) * COLS]).astype(_BF16)
            dst = k_ref if c < D_MODEL // COLS else v_ref
            dst[pl.ds(r0, TQ), _COL_SLICES[c % (D_MODEL // COLS)]] = kv

        return [norm] + [lambda c=c: chunk(c) for c in range(2 * D_MODEL // COLS)]

    def pre_steps(t, j):
        h_in = h_ref if j == 0 else out_ref

        def norm():
            hn_ref[t] = _rms(h_in[rows[t], :], ng_ref[j]).astype(_BF16)

        def query(c):
            cs = _COL_SLICES[c]
            q_ref[t, :, cs] = (_dot(hn_ref[t], w_in_ref[j, :, cs]) * _Q_SCALE).astype(_BF16)

        def gate(c):
            cs = _COL_SLICES[c]
            g = _dot(hn_ref[t], w_in_ref[j, :, D_MODEL + c * COLS:D_MODEL + (c + 1) * COLS])
            sg_ref[t, :, cs] = _silu(g).astype(_BF16)

        n_c = len(_COL_SLICES)
        return ([norm] + [lambda c=c: query(c) for c in range(n_c)]
                + [lambda c=c: gate(c) for c in range(n_c)])

    def post_steps(t, j, holder):
        h_in = h_ref if j == 0 else out_ref
        last = j == n_b - 1

        def gated():
            state = holder["state"]
            for hh, hs in enumerate(heads):
                sg_ref[t, :, hs] = state[2 * hh + 1].astype(_BF16) * sg_ref[t, :, hs]

        def mix(c):
            cs = _COL_SLICES[c]
            h1 = h_in[rows[t], cs] + _dot(sg_ref[t], w_out_ref[j, :, cs])
            out_ref[rows[t], cs] = h1
            hn_ref[t, :, cs] = h1.astype(_BF16)

        def ple(c):
            cs = _COL_SLICES[c]
            pg = _dot(hn_ref[t], gate_w_ref[j, :, cs])
            pe = _dot(p_refs[j][rows[t], :].astype(_BF16), ple_w_ref[j, :, cs])
            out_ref[rows[t], cs] = out_ref[rows[t], cs] + jax.nn.sigmoid(pg) * pe

        def final():
            out_ref[rows[t], :] = _rms(out_ref[rows[t], :], fg_ref[...])

        n_c = len(_COL_SLICES)
        steps = ([gated] + [lambda c=c: mix(c) for c in range(n_c)]
                 + [lambda c=c: ple(c) for c in range(n_c)])
        return steps + [final] if last else steps

    att = [[{} for _ in range(n_b)] for _ in range(TILES)]
    x, y = 0, 1
    for f in kv_steps(x) + pre_steps(x, 0):
        f()
    _interleave(window_steps(x, att[x][0]), kv_steps(y) + pre_steps(y, 0))
    finish_sweep(x, att[x][0])
    for j in range(n_b):
        nxt = pre_steps(x, j + 1) if j + 1 < n_b else []
        _interleave(window_steps(y, att[y][j]), post_steps(x, j, att[x][j]) + nxt)
        finish_sweep(y, att[y][j])
        if j + 1 < n_b:
            _interleave(window_steps(x, att[x][j + 1]),
                        post_steps(y, j, att[y][j]) + pre_steps(y, j + 1))
            finish_sweep(x, att[x][j + 1])
        else:
            for f in post_steps(y, j, att[y][j]):
                f()


def _b_layers(h3, p2, norm_g, w_in, w_out, gate_w, ple_w, final_g, kv_g, w_kv):
    b, s, d = h3.shape
    tm = TILES * TQ
    n_steps = s // tm
    n_b = DEPTH - N_A

    def const(shape):
        nd = len(shape)
        return pl.BlockSpec(shape, lambda bb, ii, _nd=nd: (0,) * _nd,
                            pipeline_mode=pl.Buffered(1))

    tile = lambda: pl.BlockSpec((None, tm, d), lambda bb, ii: (bb, ii, 0))
    p_spec = lambda layer: pl.BlockSpec(
        (None, tm, PLE_DIM), lambda bb, ii, _l=layer: (_l, bb * n_steps + ii, 0))
    return pl.pallas_call(
        _b_layers_kernel,
        grid=(b, n_steps),
        in_specs=[
            tile(),
            p_spec(N_A), p_spec(N_A + 1),
            const((n_b, 1, d)),
            const((n_b, d, 2 * d)),
            const((n_b, d, d)),
            const((n_b, d, d)),
            const((n_b, PLE_DIM, d)),
            const((1, d)),
            const((1, d)),
            const((d, 2 * d)),
        ],
        out_specs=tile(),
        out_shape=jax.ShapeDtypeStruct((b, s, d), _F32),
        scratch_shapes=[
            pltpu.VMEM((s, d), _BF16),
            pltpu.VMEM((s, d), _BF16),
            pltpu.VMEM((TILES, TQ, d), _BF16),
            pltpu.VMEM((TILES, TQ, d), _BF16),
            pltpu.VMEM((TILES, TQ, d), _BF16),
        ],
        compiler_params=pltpu.CompilerParams(
            dimension_semantics=("arbitrary", "arbitrary"),
            vmem_limit_bytes=VMEM_LIMIT),
        name="b_layers",
    )(h3, p2, p2, norm_g, w_in, w_out, gate_w, ple_w, final_g, kv_g, w_kv)


def kernel(x, p, norm_g, a_w_in, a_ln_g, a_ln_b, a_w_s, a_b_s, a_w_out, kv_norm_g, w_kv, b_w_in, b_w_out, ple_w, ple_gate_w, final_g):
    b, s, d = x.shape
    t = b * s
    bf = lambda w: w.astype(_BF16)
    row = lambda g: g.reshape(1, -1)

    h = x.reshape(t, d)
    p2 = p.reshape(DEPTH, t, PLE_DIM)

    for i in range(N_A):
        h = _a_layer(h, p2, i, row(norm_g[i]), bf(a_w_in[i]), row(a_ln_g[i]),
                     row(a_ln_b[i]), a_w_s[i], a_b_s[i].T, bf(a_w_out[i]),
                     bf(ple_gate_w[i]), bf(ple_w[i]))

    return _b_layers(h.reshape(b, s, d), p2, norm_g[N_A:].reshape(-1, 1, d),
                     bf(b_w_in), bf(b_w_out), bf(ple_gate_w[N_A:]), bf(ple_w[N_A:]),
                     row(final_g), row(kv_norm_g), bf(w_kv))
```

```python
import jax
import jax.numpy as jnp
import numpy as np
from jax import lax
from jax.experimental import pallas as pl
from jax.experimental.pallas import tpu as pltpu

D_MODEL = 1024
DEPTH = 4
N_A = DEPTH // 2
CHUNK = 64
GM_BLOCK = 128
A_WIDTH = 2 * D_MODEL
A_GROUPS = 8
A_GROUP_CH = A_WIDTH // A_GROUPS
B_HEADS = 8
B_HEAD_DIM = D_MODEL // B_HEADS
PLE_DIM = 256
EPS = 1e-6

COLS = 256
TM_A = 512
TQ = 256
TK = 256
TILES = 2
VMEM_LIMIT = 60 * 1024 * 1024

_SQRT_HALF = float(np.sqrt(0.5))
_Q_SCALE = float(1.0 / np.sqrt(B_HEAD_DIM))
_EXP_ZERO_CUTOFF = -105.0
_HUGE_CARRY = 1e30
_NEG_LOG2E = float(-1.0 / np.log(2.0))
_F32 = jnp.float32
_BF16 = jnp.bfloat16


def _rms(x, g):
    ms = jnp.mean(x * x, axis=-1, keepdims=True)
    return x * lax.rsqrt(ms + EPS) * g


def _gelu(x):
    return 0.5 * x * (1.0 + lax.erf(x * _SQRT_HALF))


def _silu(x):
    return x * jax.nn.sigmoid(x)


def _dot(a, b):
    return jnp.dot(a, b, preferred_element_type=_F32)


_HEAD_SLICES = [slice(hh * B_HEAD_DIM, (hh + 1) * B_HEAD_DIM) for hh in range(B_HEADS)]
_COL_SLICES = [slice(c * COLS, (c + 1) * COLS) for c in range(D_MODEL // COLS)]


def _interleave(primary, filler):
    done = 0
    for k, step in enumerate(primary):
        step()
        upto = (k + 1) * len(filler) // len(primary)
        for f in filler[done:upto]:
            f()
        done = upto


def _a_layer_kernel(h_ref, p_ref, ng_ref, w_in_ref, lng_ref, lnb_ref, ws_ref,
                    bst_ref, w_out_ref, gate_w_ref, ple_w_ref, out_ref,
                    hn_ref, v_ref, vn_ref, ug_ref, y_ref, h1b_ref):
    tm = h_ref.shape[0]
    n_grp = tm // GM_BLOCK
    chunks = [slice(c * COLS, (c + 1) * COLS) for c in range(A_WIDTH // COLS)]

    hn_ref[...] = _rms(h_ref[...], ng_ref[...]).astype(_BF16)

    stat = {"s1": jnp.zeros((tm, 1), _F32)}
    for c, cs in enumerate(chunks):
        vc = _gelu(_dot(hn_ref[...], w_in_ref[:, A_WIDTH + c * COLS:A_WIDTH + (c + 1) * COLS]))
        v_ref[:, cs] = vc
        stat["s1"] = stat["s1"] + jnp.sum(vc, axis=-1, keepdims=True)

    def mean():
        stat["mu"] = stat["s1"] * (1.0 / A_WIDTH)
        stat["s2"] = jnp.zeros((tm, 1), _F32)

    def center(cs):
        d = v_ref[:, cs] - stat["mu"]
        stat["s2"] = stat["s2"] + jnp.sum(d * d, axis=-1, keepdims=True)

    def scale():
        stat["rstd"] = lax.rsqrt(stat["s2"] * (1.0 / A_WIDTH) + EPS)

    def normalize(cs):
        vn = (v_ref[:, cs] - stat["mu"]) * stat["rstd"] * lng_ref[:, cs] + lnb_ref[:, cs]
        vn_ref[:, cs] = vn.astype(_BF16)

    def u_gate(g):
        cs = slice(g * A_GROUP_CH, (g + 1) * A_GROUP_CH)
        u = _gelu(_dot(hn_ref[...], w_in_ref[:, cs]))
        gate = _dot(hn_ref[...], w_in_ref[:, 2 * A_WIDTH + g * A_GROUP_CH:2 * A_WIDTH + (g + 1) * A_GROUP_CH])
        ug_ref[:, cs] = u * _silu(gate)

    layernorm = ([mean] + [lambda cs=cs: center(cs) for cs in chunks] + [scale]
                 + [lambda cs=cs: normalize(cs) for cs in chunks])
    _interleave([lambda g=g: u_gate(g) for g in range(A_GROUPS)], layernorm)

    t_chunk = lax.broadcasted_iota(jnp.int32, (GM_BLOCK, GM_BLOCK), 0) // CHUNK
    s_chunk = lax.broadcasted_iota(jnp.int32, (GM_BLOCK, GM_BLOCK), 1) // CHUNK
    mask = s_chunk <= t_chunk

    for g in range(A_GROUPS):
        cs = slice(g * A_GROUP_CH, (g + 1) * A_GROUP_CH)
        wm = jnp.where(mask, ws_ref[g], 0.0).astype(_BF16)
        b_col = bst_ref[:, g:g + 1]
        sv = jnp.concatenate(
            [_dot(wm, vn_ref[n * GM_BLOCK:(n + 1) * GM_BLOCK, cs]) + b_col
             for n in range(n_grp)], axis=0)
        y_ref[:, cs] = (ug_ref[:, cs] * sv).astype(_BF16)

    for cs in _COL_SLICES:
        h1 = h_ref[:, cs] + _dot(y_ref[...], w_out_ref[:, cs])
        out_ref[:, cs] = h1
        h1b_ref[:, cs] = h1.astype(_BF16)
    pb = p_ref[...].astype(_BF16)
    for cs in _COL_SLICES:
        pg = _dot(h1b_ref[...], gate_w_ref[:, cs])
        pe = _dot(pb, ple_w_ref[:, cs])
        out_ref[:, cs] = out_ref[:, cs] + jax.nn.sigmoid(pg) * pe


def _a_layer(h, p2, layer, norm_g, w_in, ln_g, ln_b, w_s, b_s_t, w_out, gate_w, ple_w):
    t = h.shape[0]
    tm = TM_A

    def param(*shape):
        return pl.BlockSpec((None,) + shape, lambda i: (layer,) + (0,) * len(shape),
                            pipeline_mode=pl.Buffered(1))

    return pl.pallas_call(
        _a_layer_kernel,
        grid=(t // tm,),
        in_specs=[
            pl.BlockSpec((tm, D_MODEL), lambda i: (i, 0)),
            pl.BlockSpec((None, tm, PLE_DIM), lambda i: (layer, i, 0)),
            param(1, D_MODEL),
            param(D_MODEL, 3 * A_WIDTH),
            param(1, A_WIDTH),
            param(1, A_WIDTH),
            param(A_GROUPS, GM_BLOCK, GM_BLOCK),
            param(GM_BLOCK, A_GROUPS),
            param(A_WIDTH, D_MODEL),
            param(D_MODEL, D_MODEL),
            param(PLE_DIM, D_MODEL),
        ],
        out_specs=pl.BlockSpec((tm, D_MODEL), lambda i: (i, 0)),
        out_shape=jax.ShapeDtypeStruct((t, D_MODEL), _F32),
        scratch_shapes=[
            pltpu.VMEM((tm, D_MODEL), _BF16),
            pltpu.VMEM((tm, A_WIDTH), _F32),
            pltpu.VMEM((tm, A_WIDTH), _BF16),
            pltpu.VMEM((tm, A_WIDTH), _F32),
            pltpu.VMEM((tm, A_WIDTH), _BF16),
            pltpu.VMEM((tm, D_MODEL), _BF16),
        ],
        compiler_params=pltpu.CompilerParams(
            dimension_semantics=("arbitrary",), vmem_limit_bytes=VMEM_LIMIT),
        name="a_layer",
    )(h, p2, norm_g, w_in, ln_g, ln_b, w_s, b_s_t, w_out, gate_w, ple_w)


def _b_layers_kernel(h_ref, pa_ref, pb_ref, ng_ref, w_in_ref, w_out_ref, gate_w_ref,
                     ple_w_ref, fg_ref, kvg_ref, w_kv_ref, out_ref,
                     k_ref, v_ref, hn_ref, q_ref, sg_ref):
    step = pl.program_id(1)
    n_b = DEPTH - N_A
    p_refs = (pa_ref, pb_ref)
    heads = _HEAD_SLICES
    rows = [slice(t * TQ, (t + 1) * TQ) for t in range(TILES)]
    blocks = [step * TILES + t for t in range(TILES)]

    half = TK // 2
    row = lax.broadcasted_iota(jnp.int32, (TK, TK), 0)
    col = lax.broadcasted_iota(jnp.int32, (TK, TK), 1)
    upper = {TK: jnp.where(row > col, 1.0, 0.0).astype(_BF16)}
    upper[half] = upper[TK][:half, :half]
    causal_top = (col < row)[:half, :half]
    causal_bottom = (col < row)[half:, :]

    def sweep_steps(items, tots, results):
        zs, nls, excl = {}, {}, {}

        def logits(k):
            q, h, k0, kn, mask = items[k][:5]
            z = lax.dot_general(q(), k_ref[pl.ds(k0, kn), heads[h]],
                                (((1,), (1,)), ((), ())), preferred_element_type=_F32)
            zs[k] = z if mask is None else jnp.where(mask, z, -_HUGE_CARRY)

        def softplus(k):
            z = zs[k]
            nls[k] = jnp.maximum(z, 0.0) + jnp.log(1.0 + jnp.exp2(jnp.abs(z) * _NEG_LOG2E))

        def suffix(k):
            excl[k] = _dot(nls[k].astype(_BF16), upper[items[k][3]])
            tots[k] = excl[k][:, 0:1] + nls[k][:, 0:1]

        def weigh(k):
            _, h, k0, kn, _, carry = items[k]
            below = nls.pop(k) + excl.pop(k)
            if carry is not None:
                below = below + carry()
            a = jnp.exp(zs.pop(k) - below)
            results[k] = _dot(a.astype(_BF16), v_ref[pl.ds(k0, kn), heads[h]])

        order = range(len(items))
        return [lambda k=k, f=f: f(k) for f in (logits, softplus, suffix, weigh) for k in order]

    def live(carries):
        low = carries[0]
        for c in carries[1:]:
            low = jnp.minimum(low, c)
        return (jnp.min(low) < -_EXP_ZERO_CUTOFF).astype(jnp.int32)

    def window_steps(t, holder):
        blk = blocks[t]
        start = pl.multiple_of(blk * TK, TK)
        prev = pl.multiple_of(jnp.maximum(blk - 1, 0) * TK, TK)
        no_prev = jnp.where(blk > 0, 0.0, _HUGE_CARRY)
        d_items, d_tots, d_pvs, diag = [], {}, {}, {}
        for h, hs in enumerate(heads):
            d_items.append((lambda hs=hs: q_ref[t, :half, hs], h, start, half, causal_top, None))
            d_items.append((lambda hs=hs: q_ref[t, half:, hs], h, start, TK, causal_bottom, None))

        def diag_carry(h):
            if h not in diag:
                diag[h] = jnp.concatenate([d_tots[2 * h], d_tots[2 * h + 1]], axis=0) + no_prev
            return diag[h]

        p_items = [(lambda hs=hs: q_ref[t, :, hs], h, prev, TK, None,
                    lambda h=h: diag_carry(h)) for h, hs in enumerate(heads)]
        p_tots, p_pvs = {}, {}
        d_steps = sweep_steps(d_items, d_tots, d_pvs)
        p_steps = sweep_steps(p_items, p_tots, p_pvs)
        nd, npv = len(d_items), len(p_items)

        def carries():
            holder["carry"] = [diag_carry(h) + p_tots[h] for h in range(B_HEADS)]
            holder["go"] = live(holder["carry"])

        def finish():
            holder["acc"] = [
                jnp.concatenate([d_pvs[2 * h], d_pvs[2 * h + 1]], axis=0) + p_pvs[h]
                for h in range(B_HEADS)]

        ordered = []
        for stage in range(4):
            ordered += d_steps[stage * nd:(stage + 1) * nd] + p_steps[stage * npv:(stage + 1) * npv]
            if stage == 2:
                ordered.append(carries)
        return ordered + [finish]

    def finish_sweep(t, holder):
        def full_block(j, carry, acc):
            k0 = pl.multiple_of(j * TK, TK)
            tots, pvs = {}, {}
            items = [(lambda hs=hs: q_ref[t, :, hs], h, k0, TK, None,
                      lambda h=h: carry[h]) for h, hs in enumerate(heads)]
            for f in sweep_steps(items, tots, pvs):
                f()
            return ([carry[h] + tots[h] for h in range(B_HEADS)],
                    [acc[h] + pvs[h] for h in range(B_HEADS)])

        def cond(loop):
            return jnp.logical_and(loop[0] >= 0, loop[1] > 0)

        def body(loop):
            carry, acc = full_block(loop[0], loop[2], loop[3])
            return loop[0] - 1, live(carry), carry, acc

        out = lax.while_loop(cond, body, (blocks[t] - 2, holder["go"], holder["carry"], holder["acc"]))
        holder["acc"] = out[3]

    def kv_steps(t):
        r0 = pl.multiple_of(blocks[t] * TQ, TQ)

        def norm():
            hn_ref[t] = _rms(h_ref[rows[t], :], kvg_ref[...]).astype(_BF16)

        def chunk(c):
            kv = _dot(hn_ref[t], w_kv_ref[:, c * COLS:(c + 1) * COLS]).astype(_BF16)
            dst = k_ref if c < D_MODEL // COLS else v_ref
            dst[pl.ds(r0, TQ), _COL_SLICES[c % (D_MODEL // COLS)]] = kv

        return [norm] + [lambda c=c: chunk(c) for c in range(2 * D_MODEL // COLS)]

    def pre_steps(t, j):
        h_in = h_ref if j == 0 else out_ref

        def norm():
            hn_ref[t] = _rms(h_in[rows[t], :], ng_ref[j]).astype(_BF16)

        def query(c):
            cs = _COL_SLICES[c]
            q_ref[t, :, cs] = (_dot(hn_ref[t], w_in_ref[j, :, cs]) * _Q_SCALE).astype(_BF16)

        def gate(c):
            cs = _COL_SLICES[c]
            g = _dot(hn_ref[t], w_in_ref[j, :, D_MODEL + c * COLS:D_MODEL + (c + 1) * COLS])
            sg_ref[t, :, cs] = _silu(g).astype(_BF16)

        n_c = len(_COL_SLICES)
        return ([norm] + [lambda c=c: query(c) for c in range(n_c)]
                + [lambda c=c: gate(c) for c in range(n_c)])

    def post_steps(t, j, holder):
        h_in = h_ref if j == 0 else out_ref
        last = j == n_b - 1

        def gated():
            for hh, hs in enumerate(heads):
                sg_ref[t, :, hs] = holder["acc"][hh].astype(_BF16) * sg_ref[t, :, hs]

        def mix(c):
            cs = _COL_SLICES[c]
            h1 = h_in[rows[t], cs] + _dot(sg_ref[t], w_out_ref[j, :, cs])
            out_ref[rows[t], cs] = h1
            hn_ref[t, :, cs] = h1.astype(_BF16)

        def ple(c):
            cs = _COL_SLICES[c]
            pg = _dot(hn_ref[t], gate_w_ref[j, :, cs])
            pe = _dot(p_refs[j][rows[t], :].astype(_BF16), ple_w_ref[j, :, cs])
            out_ref[rows[t], cs] = out_ref[rows[t], cs] + jax.nn.sigmoid(pg) * pe

        def final():
            out_ref[rows[t], :] = _rms(out_ref[rows[t], :], fg_ref[...])

        n_c = len(_COL_SLICES)
        steps = ([gated] + [lambda c=c: mix(c) for c in range(n_c)]
                 + [lambda c=c: ple(c) for c in range(n_c)])
        return steps + [final] if last else steps

    att = [[{} for _ in range(n_b)] for _ in range(TILES)]
    x, y = 0, 1
    for f in kv_steps(x) + pre_steps(x, 0):
        f()
    _interleave(window_steps(x, att[x][0]), kv_steps(y) + pre_steps(y, 0))
    finish_sweep(x, att[x][0])
    for j in range(n_b):
        nxt = pre_steps(x, j + 1) if j + 1 < n_b else []
        _interleave(window_steps(y, att[y][j]), post_steps(x, j, att[x][j]) + nxt)
        finish_sweep(y, att[y][j])
        if j + 1 < n_b:
            _interleave(window_steps(x, att[x][j + 1]),
                        post_steps(y, j, att[y][j]) + pre_steps(y, j + 1))
            finish_sweep(x, att[x][j + 1])
        else:
            for f in post_steps(y, j, att[y][j]):
                f()


def _b_layers(h3, p2, norm_g, w_in, w_out, gate_w, ple_w, final_g, kv_g, w_kv):
    b, s, d = h3.shape
    tm = TILES * TQ
    n_steps = s // tm
    n_b = DEPTH - N_A

    def const(shape):
        nd = len(shape)
        return pl.BlockSpec(shape, lambda bb, ii, _nd=nd: (0,) * _nd,
                            pipeline_mode=pl.Buffered(1))

    def upper_half(shape):
        nd = len(shape)
        return pl.BlockSpec(shape, lambda bb, ii, _nd=nd: (N_A // n_b,) + (0,) * (_nd - 1),
                            pipeline_mode=pl.Buffered(1))

    tile = lambda: pl.BlockSpec((None, tm, d), lambda bb, ii: (bb, ii, 0))
    p_spec = lambda layer: pl.BlockSpec(
        (None, tm, PLE_DIM), lambda bb, ii, _l=layer: (_l, bb * n_steps + ii, 0))
    return pl.pallas_call(
        _b_layers_kernel,
        grid=(b, n_steps),
        in_specs=[
            tile(),
            p_spec(N_A), p_spec(N_A + 1),
            upper_half((n_b, 1, d)),
            const((n_b, d, 2 * d)),
            const((n_b, d, d)),
            upper_half((n_b, d, d)),
            upper_half((n_b, PLE_DIM, d)),
            const((1, d)),
            const((1, d)),
            const((d, 2 * d)),
        ],
        out_specs=tile(),
        out_shape=jax.ShapeDtypeStruct((b, s, d), _F32),
        scratch_shapes=[
            pltpu.VMEM((s, d), _BF16),
            pltpu.VMEM((s, d), _BF16),
            pltpu.VMEM((TILES, TQ, d), _BF16),
            pltpu.VMEM((TILES, TQ, d), _BF16),
            pltpu.VMEM((TILES, TQ, d), _BF16),
        ],
        compiler_params=pltpu.CompilerParams(
            dimension_semantics=("arbitrary", "arbitrary"),
            vmem_limit_bytes=VMEM_LIMIT),
        name="b_layers",
    )(h3, p2, p2, norm_g, w_in, w_out, gate_w, ple_w, final_g, kv_g, w_kv)


def kernel(x, p, norm_g, a_w_in, a_ln_g, a_ln_b, a_w_s, a_b_s, a_w_out, kv_norm_g, w_kv, b_w_in, b_w_out, ple_w, ple_gate_w, final_g):
    b, s, d = x.shape
    t = b * s
    bf = lambda w: w.astype(_BF16)

    h = x.reshape(t, d)
    p2 = p.reshape(DEPTH, t, PLE_DIM)
    norm_g3 = norm_g.reshape(DEPTH, 1, d)
    gate_w, ple_wb = bf(ple_gate_w), bf(ple_w)

    a_params = (norm_g3, bf(a_w_in), a_ln_g.reshape(N_A, 1, -1), a_ln_b.reshape(N_A, 1, -1),
                a_w_s, jnp.swapaxes(a_b_s, 1, 2), bf(a_w_out), gate_w, ple_wb)
    for i in range(N_A):
        h = _a_layer(h, p2, i, *a_params)

    return _b_layers(h.reshape(b, s, d), p2, norm_g3, bf(b_w_in), bf(b_w_out), gate_w, ple_wb,
                     final_g.reshape(1, d), kv_norm_g.reshape(1, d), bf(w_kv))
```

```python
import jax
import jax.numpy as jnp
import numpy as np
from jax import lax
from jax.experimental import pallas as pl
from jax.experimental.pallas import tpu as pltpu

D_MODEL = 1024
DEPTH = 4
N_A = DEPTH // 2
CHUNK = 64
GM_BLOCK = 128
A_WIDTH = 2 * D_MODEL
A_GROUPS = 8
A_GROUP_CH = A_WIDTH // A_GROUPS
B_HEADS = 8
B_HEAD_DIM = D_MODEL // B_HEADS
PLE_DIM = 256
EPS = 1e-6

COLS = 256
TM_A = 512
TQ = 256
TK = 256
TILES = 2
HEAD_GROUPS = 2
VMEM_LIMIT = 60 * 1024 * 1024

_SQRT_HALF = float(np.sqrt(0.5))
_Q_SCALE = float(1.0 / np.sqrt(B_HEAD_DIM))
_EXP_ZERO_CUTOFF = -105.0
_HUGE_CARRY = 1e30
_NEG_LOG2E = float(-1.0 / np.log(2.0))
_F32 = jnp.float32
_BF16 = jnp.bfloat16


def _rms(x, g):
    ms = jnp.mean(x * x, axis=-1, keepdims=True)
    return x * lax.rsqrt(ms + EPS) * g


def _gelu(x):
    return 0.5 * x * (1.0 + lax.erf(x * _SQRT_HALF))


def _silu(x):
    return x * jax.nn.sigmoid(x)


def _dot(a, b):
    return jnp.dot(a, b, preferred_element_type=_F32)


_HEAD_SLICES = [slice(hh * B_HEAD_DIM, (hh + 1) * B_HEAD_DIM) for hh in range(B_HEADS)]
_COL_SLICES = [slice(c * COLS, (c + 1) * COLS) for c in range(D_MODEL // COLS)]


def _interleave(primary, filler):
    done = 0
    for k, step in enumerate(primary):
        step()
        upto = (k + 1) * len(filler) // len(primary)
        for f in filler[done:upto]:
            f()
        done = upto


def _a_layer_kernel(h_ref, p_ref, ng_ref, w_in_ref, lng_ref, lnb_ref, ws_ref,
                    bst_ref, w_out_ref, gate_w_ref, ple_w_ref, out_ref,
                    hn_ref, v_ref, vn_ref, ug_ref, y_ref, h1b_ref):
    tm = h_ref.shape[0]
    n_grp = tm // GM_BLOCK
    chunks = [slice(c * COLS, (c + 1) * COLS) for c in range(A_WIDTH // COLS)]

    hn_ref[...] = _rms(h_ref[...], ng_ref[...]).astype(_BF16)

    stat = {"s1": jnp.zeros((tm, 1), _F32)}
    for c, cs in enumerate(chunks):
        vc = _gelu(_dot(hn_ref[...], w_in_ref[:, A_WIDTH + c * COLS:A_WIDTH + (c + 1) * COLS]))
        v_ref[:, cs] = vc
        stat["s1"] = stat["s1"] + jnp.sum(vc, axis=-1, keepdims=True)

    def mean():
        stat["mu"] = stat["s1"] * (1.0 / A_WIDTH)
        stat["s2"] = jnp.zeros((tm, 1), _F32)

    def center(cs):
        d = v_ref[:, cs] - stat["mu"]
        stat["s2"] = stat["s2"] + jnp.sum(d * d, axis=-1, keepdims=True)

    def scale():
        stat["rstd"] = lax.rsqrt(stat["s2"] * (1.0 / A_WIDTH) + EPS)

    def normalize(cs):
        vn = (v_ref[:, cs] - stat["mu"]) * stat["rstd"] * lng_ref[:, cs] + lnb_ref[:, cs]
        vn_ref[:, cs] = vn.astype(_BF16)

    def u_gate(g):
        cs = slice(g * A_GROUP_CH, (g + 1) * A_GROUP_CH)
        u = _gelu(_dot(hn_ref[...], w_in_ref[:, cs]))
        gate = _dot(hn_ref[...], w_in_ref[:, 2 * A_WIDTH + g * A_GROUP_CH:2 * A_WIDTH + (g + 1) * A_GROUP_CH])
        ug_ref[:, cs] = u * _silu(gate)

    layernorm = ([mean] + [lambda cs=cs: center(cs) for cs in chunks] + [scale]
                 + [lambda cs=cs: normalize(cs) for cs in chunks])
    _interleave([lambda g=g: u_gate(g) for g in range(A_GROUPS)], layernorm)

    t_chunk = lax.broadcasted_iota(jnp.int32, (GM_BLOCK, GM_BLOCK), 0) // CHUNK
    s_chunk = lax.broadcasted_iota(jnp.int32, (GM_BLOCK, GM_BLOCK), 1) // CHUNK
    mask = s_chunk <= t_chunk

    for g in range(A_GROUPS):
        cs = slice(g * A_GROUP_CH, (g + 1) * A_GROUP_CH)
        wm = jnp.where(mask, ws_ref[g], 0.0).astype(_BF16)
        b_col = bst_ref[:, g:g + 1]
        sv = jnp.concatenate(
            [_dot(wm, vn_ref[n * GM_BLOCK:(n + 1) * GM_BLOCK, cs]) + b_col
             for n in range(n_grp)], axis=0)
        y_ref[:, cs] = (ug_ref[:, cs] * sv).astype(_BF16)

    for cs in _COL_SLICES:
        h1 = h_ref[:, cs] + _dot(y_ref[...], w_out_ref[:, cs])
        out_ref[:, cs] = h1
        h1b_ref[:, cs] = h1.astype(_BF16)
    pb = p_ref[...].astype(_BF16)
    for cs in _COL_SLICES:
        pg = _dot(h1b_ref[...], gate_w_ref[:, cs])
        pe = _dot(pb, ple_w_ref[:, cs])
        out_ref[:, cs] = out_ref[:, cs] + jax.nn.sigmoid(pg) * pe


def _a_layer(h, p2, layer, norm_g, w_in, ln_g, ln_b, w_s, b_s_t, w_out, gate_w, ple_w):
    t = h.shape[0]
    tm = TM_A

    def param(*shape):
        return pl.BlockSpec((None,) + shape, lambda i: (layer,) + (0,) * len(shape),
                            pipeline_mode=pl.Buffered(1))

    return pl.pallas_call(
        _a_layer_kernel,
        grid=(t // tm,),
        in_specs=[
            pl.BlockSpec((tm, D_MODEL), lambda i: (i, 0)),
            pl.BlockSpec((None, tm, PLE_DIM), lambda i: (layer, i, 0)),
            param(1, D_MODEL),
            param(D_MODEL, 3 * A_WIDTH),
            param(1, A_WIDTH),
            param(1, A_WIDTH),
            param(A_GROUPS, GM_BLOCK, GM_BLOCK),
            param(GM_BLOCK, A_GROUPS),
            param(A_WIDTH, D_MODEL),
            param(D_MODEL, D_MODEL),
            param(PLE_DIM, D_MODEL),
        ],
        out_specs=pl.BlockSpec((tm, D_MODEL), lambda i: (i, 0)),
        out_shape=jax.ShapeDtypeStruct((t, D_MODEL), _F32),
        scratch_shapes=[
            pltpu.VMEM((tm, D_MODEL), _BF16),
            pltpu.VMEM((tm, A_WIDTH), _F32),
            pltpu.VMEM((tm, A_WIDTH), _BF16),
            pltpu.VMEM((tm, A_WIDTH), _F32),
            pltpu.VMEM((tm, A_WIDTH), _BF16),
            pltpu.VMEM((tm, D_MODEL), _BF16),
        ],
        compiler_params=pltpu.CompilerParams(
            dimension_semantics=("arbitrary",), vmem_limit_bytes=VMEM_LIMIT),
        name="a_layer",
    )(h, p2, norm_g, w_in, ln_g, ln_b, w_s, b_s_t, w_out, gate_w, ple_w)


def _b_layers_kernel(h_ref, p_ref, ng_ref, w_in_ref, w_out_ref, gate_w_ref,
                     ple_w_ref, fg_ref, kvg_ref, w_kv_ref, out_ref,
                     k_ref, v_ref, hn_ref, q_ref, sg_ref):
    step = pl.program_id(1)
    n_b = DEPTH - N_A
    heads = _HEAD_SLICES
    rows = [slice(t * TQ, (t + 1) * TQ) for t in range(TILES)]
    blocks = [step * TILES + t for t in range(TILES)]

    half = TK // 2
    row = lax.broadcasted_iota(jnp.int32, (TK, TK), 0)
    col = lax.broadcasted_iota(jnp.int32, (TK, TK), 1)
    upper = {TK: jnp.where(row > col, 1.0, 0.0).astype(_BF16)}
    upper[half] = upper[TK][:half, :half]
    causal_top = (col < row)[:half, :half]
    causal_bottom = (col < row)[half:, :]

    def sweep_steps(items, tots, results):
        zs, nls, excl = {}, {}, {}

        def logits(k):
            q, h, k0, kn, mask = items[k][:5]
            z = lax.dot_general(q(), k_ref[pl.ds(k0, kn), heads[h]],
                                (((1,), (1,)), ((), ())), preferred_element_type=_F32)
            zs[k] = z if mask is None else jnp.where(mask, z, -_HUGE_CARRY)

        def softplus(k):
            z = zs[k]
            nls[k] = jnp.maximum(z, 0.0) + jnp.log(1.0 + jnp.exp2(jnp.abs(z) * _NEG_LOG2E))

        def suffix(k):
            excl[k] = _dot(nls[k].astype(_BF16), upper[items[k][3]])
            tots[k] = excl[k][:, 0:1] + nls[k][:, 0:1]

        def weigh(k):
            _, h, k0, kn, _, carry = items[k]
            below = nls.pop(k) + excl.pop(k)
            if carry is not None:
                below = below + carry()
            a = jnp.exp(zs.pop(k) - below)
            results[k] = _dot(a.astype(_BF16), v_ref[pl.ds(k0, kn), heads[h]])

        order = range(len(items))
        return [lambda k=k, f=f: f(k) for f in (logits, softplus, suffix, weigh) for k in order]

    def live(carries):
        low = carries[0]
        for c in carries[1:]:
            low = jnp.minimum(low, c)
        return (jnp.min(low) < -_EXP_ZERO_CUTOFF).astype(jnp.int32)

    def window_steps(t, holder):
        blk = blocks[t]
        start = pl.multiple_of(blk * TK, TK)
        prev = pl.multiple_of(jnp.maximum(blk - 1, 0) * TK, TK)
        no_prev = jnp.where(blk > 0, 0.0, _HUGE_CARRY)
        d_items, d_tots, d_pvs, diag = [], {}, {}, {}
        for h, hs in enumerate(heads):
            d_items.append((lambda hs=hs: q_ref[t, :half, hs], h, start, half, causal_top, None))
            d_items.append((lambda hs=hs: q_ref[t, half:, hs], h, start, TK, causal_bottom, None))

        def diag_carry(h):
            if h not in diag:
                diag[h] = jnp.concatenate([d_tots[2 * h], d_tots[2 * h + 1]], axis=0) + no_prev
            return diag[h]

        p_items = [(lambda hs=hs: q_ref[t, :, hs], h, prev, TK, None,
                    lambda h=h: diag_carry(h)) for h, hs in enumerate(heads)]
        p_tots, p_pvs = {}, {}
        d_steps = sweep_steps(d_items, d_tots, d_pvs)
        p_steps = sweep_steps(p_items, p_tots, p_pvs)
        nd, npv = len(d_items), len(p_items)

        def carries():
            holder["carry"] = [diag_carry(h) + p_tots[h] for h in range(B_HEADS)]
            holder["go"] = live(holder["carry"])

        def finish():
            holder["acc"] = [
                jnp.concatenate([d_pvs[2 * h], d_pvs[2 * h + 1]], axis=0) + p_pvs[h]
                for h in range(B_HEADS)]

        ordered = []
        per = B_HEADS // HEAD_GROUPS
        for grp in range(HEAD_GROUPS):
            hs_ = range(grp * per, (grp + 1) * per)
            for stage in range(4):
                for h in hs_:
                    ordered += d_steps[stage * nd + 2 * h:stage * nd + 2 * h + 2]
                for h in hs_:
                    ordered.append(p_steps[stage * npv + h])
                if stage == 2 and grp == HEAD_GROUPS - 1:
                    ordered.append(carries)
        return ordered + [finish]

    def finish_sweep(t, holder):
        def full_block(j, carry, acc):
            k0 = pl.multiple_of(j * TK, TK)
            tots, pvs = {}, {}
            items = [(lambda hs=hs: q_ref[t, :, hs], h, k0, TK, None,
                      lambda h=h: carry[h]) for h, hs in enumerate(heads)]
            for f in sweep_steps(items, tots, pvs):
                f()
            return ([carry[h] + tots[h] for h in range(B_HEADS)],
                    [acc[h] + pvs[h] for h in range(B_HEADS)])

        def cond(loop):
            return jnp.logical_and(loop[0] >= 0, loop[1] > 0)

        def body(loop):
            carry, acc = full_block(loop[0], loop[2], loop[3])
            return loop[0] - 1, live(carry), carry, acc

        out = lax.while_loop(cond, body, (blocks[t] - 2, holder["go"], holder["carry"], holder["acc"]))
        holder["acc"] = out[3]

    def kv_steps(t):
        r0 = pl.multiple_of(blocks[t] * TQ, TQ)

        def norm():
            hn_ref[t] = _rms(out_ref[rows[t], :], kvg_ref[...]).astype(_BF16)

        def chunk(c):
            kv = _dot(hn_ref[t], w_kv_ref[:, c * COLS:(c + 1) * COLS]).astype(_BF16)
            dst = k_ref if c < D_MODEL // COLS else v_ref
            dst[pl.ds(r0, TQ), _COL_SLICES[c % (D_MODEL // COLS)]] = kv

        return [norm] + [lambda c=c: chunk(c) for c in range(2 * D_MODEL // COLS)]

    def pre_steps(t, j):
        def norm():
            hn_ref[t] = _rms(out_ref[rows[t], :], ng_ref[j]).astype(_BF16)

        def query(c):
            cs = _COL_SLICES[c]
            q_ref[t, :, cs] = (_dot(hn_ref[t], w_in_ref[j, :, cs]) * _Q_SCALE).astype(_BF16)

        def gate(c):
            cs = _COL_SLICES[c]
            g = _dot(hn_ref[t], w_in_ref[j, :, D_MODEL + c * COLS:D_MODEL + (c + 1) * COLS])
            sg_ref[t, :, cs] = _silu(g).astype(_BF16)

        n_c = len(_COL_SLICES)
        return ([norm] + [lambda c=c: query(c) for c in range(n_c)]
                + [lambda c=c: gate(c) for c in range(n_c)])

    def post_steps(t, j, holder):
        def gated():
            for hh, hs in enumerate(heads):
                sg_ref[t, :, hs] = holder["acc"][hh].astype(_BF16) * sg_ref[t, :, hs]

        def mix(c):
            cs = _COL_SLICES[c]
            h1 = out_ref[rows[t], cs] + _dot(sg_ref[t], w_out_ref[j, :, cs])
            out_ref[rows[t], cs] = h1
            hn_ref[t, :, cs] = h1.astype(_BF16)

        def ple(c):
            cs = _COL_SLICES[c]
            pg = _dot(hn_ref[t], gate_w_ref[j, :, cs])
            pe = _dot(p_ref[j, rows[t], :].astype(_BF16), ple_w_ref[j, :, cs])
            out_ref[rows[t], cs] = out_ref[rows[t], cs] + jax.nn.sigmoid(pg) * pe

        n_c = len(_COL_SLICES)
        return ([gated] + [lambda c=c: mix(c) for c in range(n_c)]
                + [lambda c=c: ple(c) for c in range(n_c)])

    def program(sweep):
        x, y = 0, 1

        def settle(t, holder, more):
            if sweep:
                finish_sweep(t, holder)
                return more
            return jnp.maximum(more, jnp.where(blocks[t] >= 2, holder["go"], 0))

        def layer(j, more):
            att = [{} for _ in range(TILES)]
            for f in pre_steps(x, j):
                f()
            _interleave(window_steps(x, att[x]), pre_steps(y, j))
            more = settle(x, att[x], more)
            _interleave(window_steps(y, att[y]), post_steps(x, j, att[x]))
            more = settle(y, att[y], more)
            for f in post_steps(y, j, att[y]):
                f()
            return more

        out_ref[...] = h_ref[...]
        for f in kv_steps(x) + kv_steps(y):
            f()
        more = lax.fori_loop(0, n_b, layer, jnp.int32(0))
        for t in range(TILES):
            out_ref[rows[t], :] = _rms(out_ref[rows[t], :], fg_ref[...])
        return more

    redo = program(sweep=False)

    @pl.when(redo > 0)
    def _():
        program(sweep=True)


def _b_layers(h3, p2, norm_g, w_in, w_out, gate_w, ple_w, final_g, kv_g, w_kv):
    b, s, d = h3.shape
    tm = TILES * TQ
    n_steps = s // tm
    n_b = DEPTH - N_A

    def const(shape):
        nd = len(shape)
        return pl.BlockSpec(shape, lambda bb, ii, _nd=nd: (0,) * _nd,
                            pipeline_mode=pl.Buffered(1))

    def upper_half(shape):
        nd = len(shape)
        return pl.BlockSpec(shape, lambda bb, ii, _nd=nd: (N_A // n_b,) + (0,) * (_nd - 1),
                            pipeline_mode=pl.Buffered(1))

    tile = lambda: pl.BlockSpec((None, tm, d), lambda bb, ii: (bb, ii, 0))
    return pl.pallas_call(
        _b_layers_kernel,
        grid=(b, n_steps),
        in_specs=[
            tile(),
            pl.BlockSpec((n_b, tm, PLE_DIM), lambda bb, ii: (N_A // n_b, bb * n_steps + ii, 0)),
            upper_half((n_b, 1, d)),
            const((n_b, d, 2 * d)),
            const((n_b, d, d)),
            upper_half((n_b, d, d)),
            upper_half((n_b, PLE_DIM, d)),
            const((1, d)),
            const((1, d)),
            const((d, 2 * d)),
        ],
        out_specs=tile(),
        out_shape=jax.ShapeDtypeStruct((b, s, d), _F32),
        scratch_shapes=[
            pltpu.VMEM((s, d), _BF16),
            pltpu.VMEM((s, d), _BF16),
            pltpu.VMEM((TILES, TQ, d), _BF16),
            pltpu.VMEM((TILES, TQ, d), _BF16),
            pltpu.VMEM((TILES, TQ, d), _BF16),
        ],
        compiler_params=pltpu.CompilerParams(
            dimension_semantics=("arbitrary", "arbitrary"),
            vmem_limit_bytes=VMEM_LIMIT),
        name="b_layers",
    )(h3, p2, norm_g, w_in, w_out, gate_w, ple_w, final_g, kv_g, w_kv)


def kernel(x, p, norm_g, a_w_in, a_ln_g, a_ln_b, a_w_s, a_b_s, a_w_out, kv_norm_g, w_kv, b_w_in, b_w_out, ple_w, ple_gate_w, final_g):
    b, s, d = x.shape
    t = b * s
    bf = lambda w: w.astype(_BF16)

    h = x.reshape(t, d)
    p2 = p.reshape(DEPTH, t, PLE_DIM)
    norm_g3 = norm_g.reshape(DEPTH, 1, d)
    gate_w, ple_wb = bf(ple_gate_w), bf(ple_w)

    a_params = (norm_g3, bf(a_w_in), a_ln_g.reshape(N_A, 1, -1), a_ln_b.reshape(N_A, 1, -1),
                a_w_s, jnp.swapaxes(a_b_s, 1, 2), bf(a_w_out), gate_w, ple_wb)
    for i in range(N_A):
        h = _a_layer(h, p2, i, *a_params)

    return _b_layers(h.reshape(b, s, d), p2, norm_g3, bf(b_w_in), bf(b_w_out), gate_w, ple_wb,
                     final_g.reshape(1, d), kv_norm_g.reshape(1, d), bf(w_kv))
```

```python
import jax
import jax.numpy as jnp
import numpy as np
from jax import lax
from jax.experimental import pallas as pl
from jax.experimental.pallas import tpu as pltpu

D_MODEL = 1024
DEPTH = 4
N_A = DEPTH // 2
CHUNK = 64
GM_BLOCK = 128
A_WIDTH = 2 * D_MODEL
A_GROUPS = 8
A_GROUP_CH = A_WIDTH // A_GROUPS
B_HEADS = 8
B_HEAD_DIM = D_MODEL // B_HEADS
PLE_DIM = 256
EPS = 1e-6

COLS = 256
TM_A = 512
TQ = 256
TK = 256
TILES = 2
HEAD_GROUPS = 4
VMEM_LIMIT = 60 * 1024 * 1024

_SQRT_HALF = float(np.sqrt(0.5))
_Q_SCALE = float(1.0 / np.sqrt(B_HEAD_DIM))
_EXP_ZERO_CUTOFF = -105.0
_HUGE_CARRY = 1e30
_NEG_LOG2E = float(-1.0 / np.log(2.0))
_F32 = jnp.float32
_BF16 = jnp.bfloat16


def _rms(x, g):
    ms = jnp.mean(x * x, axis=-1, keepdims=True)
    return x * lax.rsqrt(ms + EPS) * g


def _gelu(x):
    return 0.5 * x * (1.0 + lax.erf(x * _SQRT_HALF))


def _silu(x):
    return x * jax.nn.sigmoid(x)


def _dot(a, b):
    return jnp.dot(a, b, preferred_element_type=_F32)


_HEAD_SLICES = [slice(hh * B_HEAD_DIM, (hh + 1) * B_HEAD_DIM) for hh in range(B_HEADS)]
_COL_SLICES = [slice(c * COLS, (c + 1) * COLS) for c in range(D_MODEL // COLS)]


def _interleave(primary, filler):
    done = 0
    for k, step in enumerate(primary):
        step()
        upto = (k + 1) * len(filler) // len(primary)
        for f in filler[done:upto]:
            f()
        done = upto


def _a_layer_kernel(h_ref, p_ref, ng_ref, w_in_ref, lng_ref, lnb_ref, ws_ref,
                    bst_ref, w_out_ref, gate_w_ref, ple_w_ref, out_ref,
                    hn_ref, v_ref, vn_ref, ug_ref, y_ref, h1b_ref):
    tm = h_ref.shape[0]
    n_grp = tm // GM_BLOCK
    chunks = [slice(c * COLS, (c + 1) * COLS) for c in range(A_WIDTH // COLS)]

    hn_ref[...] = _rms(h_ref[...], ng_ref[...]).astype(_BF16)

    stat = {"s1": jnp.zeros((tm, 1), _F32)}
    for c, cs in enumerate(chunks):
        vc = _gelu(_dot(hn_ref[...], w_in_ref[:, A_WIDTH + c * COLS:A_WIDTH + (c + 1) * COLS]))
        v_ref[:, cs] = vc
        stat["s1"] = stat["s1"] + jnp.sum(vc, axis=-1, keepdims=True)

    def mean():
        stat["mu"] = stat["s1"] * (1.0 / A_WIDTH)
        stat["s2"] = jnp.zeros((tm, 1), _F32)

    def center(cs):
        d = v_ref[:, cs] - stat["mu"]
        stat["s2"] = stat["s2"] + jnp.sum(d * d, axis=-1, keepdims=True)

    def scale():
        stat["rstd"] = lax.rsqrt(stat["s2"] * (1.0 / A_WIDTH) + EPS)

    def normalize(cs):
        vn = (v_ref[:, cs] - stat["mu"]) * stat["rstd"] * lng_ref[:, cs] + lnb_ref[:, cs]
        vn_ref[:, cs] = vn.astype(_BF16)

    def u_gate(g):
        cs = slice(g * A_GROUP_CH, (g + 1) * A_GROUP_CH)
        u = _gelu(_dot(hn_ref[...], w_in_ref[:, cs]))
        gate = _dot(hn_ref[...], w_in_ref[:, 2 * A_WIDTH + g * A_GROUP_CH:2 * A_WIDTH + (g + 1) * A_GROUP_CH])
        ug_ref[:, cs] = u * _silu(gate)

    layernorm = ([mean] + [lambda cs=cs: center(cs) for cs in chunks] + [scale]
                 + [lambda cs=cs: normalize(cs) for cs in chunks])
    _interleave([lambda g=g: u_gate(g) for g in range(A_GROUPS)], layernorm)

    t_chunk = lax.broadcasted_iota(jnp.int32, (GM_BLOCK, GM_BLOCK), 0) // CHUNK
    s_chunk = lax.broadcasted_iota(jnp.int32, (GM_BLOCK, GM_BLOCK), 1) // CHUNK
    mask = s_chunk <= t_chunk

    for g in range(A_GROUPS):
        cs = slice(g * A_GROUP_CH, (g + 1) * A_GROUP_CH)
        wm = jnp.where(mask, ws_ref[g], 0.0).astype(_BF16)
        b_col = bst_ref[:, g:g + 1]
        sv = jnp.concatenate(
            [_dot(wm, vn_ref[n * GM_BLOCK:(n + 1) * GM_BLOCK, cs]) + b_col
             for n in range(n_grp)], axis=0)
        y_ref[:, cs] = (ug_ref[:, cs] * sv).astype(_BF16)

    for cs in _COL_SLICES:
        h1 = h_ref[:, cs] + _dot(y_ref[...], w_out_ref[:, cs])
        out_ref[:, cs] = h1
        h1b_ref[:, cs] = h1.astype(_BF16)
    pb = p_ref[...].astype(_BF16)
    for cs in _COL_SLICES:
        pg = _dot(h1b_ref[...], gate_w_ref[:, cs])
        pe = _dot(pb, ple_w_ref[:, cs])
        out_ref[:, cs] = out_ref[:, cs] + jax.nn.sigmoid(pg) * pe


def _a_layer(h, p2, layer, norm_g, w_in, ln_g, ln_b, w_s, b_s_t, w_out, gate_w, ple_w):
    t = h.shape[0]
    tm = TM_A

    def param(*shape):
        return pl.BlockSpec((None,) + shape, lambda i: (layer,) + (0,) * len(shape),
                            pipeline_mode=pl.Buffered(1))

    return pl.pallas_call(
        _a_layer_kernel,
        grid=(t // tm,),
        in_specs=[
            pl.BlockSpec((tm, D_MODEL), lambda i: (i, 0)),
            pl.BlockSpec((None, tm, PLE_DIM), lambda i: (layer, i, 0)),
            param(1, D_MODEL),
            param(D_MODEL, 3 * A_WIDTH),
            param(1, A_WIDTH),
            param(1, A_WIDTH),
            param(A_GROUPS, GM_BLOCK, GM_BLOCK),
            param(GM_BLOCK, A_GROUPS),
            param(A_WIDTH, D_MODEL),
            param(D_MODEL, D_MODEL),
            param(PLE_DIM, D_MODEL),
        ],
        out_specs=pl.BlockSpec((tm, D_MODEL), lambda i: (i, 0)),
        out_shape=jax.ShapeDtypeStruct((t, D_MODEL), _F32),
        scratch_shapes=[
            pltpu.VMEM((tm, D_MODEL), _BF16),
            pltpu.VMEM((tm, A_WIDTH), _F32),
            pltpu.VMEM((tm, A_WIDTH), _BF16),
            pltpu.VMEM((tm, A_WIDTH), _F32),
            pltpu.VMEM((tm, A_WIDTH), _BF16),
            pltpu.VMEM((tm, D_MODEL), _BF16),
        ],
        compiler_params=pltpu.CompilerParams(
            dimension_semantics=("arbitrary",), vmem_limit_bytes=VMEM_LIMIT),
        name="a_layer",
    )(h, p2, norm_g, w_in, ln_g, ln_b, w_s, b_s_t, w_out, gate_w, ple_w)


def _b_layers_kernel(h_ref, p_ref, ng_ref, w_in_ref, w_out_ref, gate_w_ref,
                     ple_w_ref, fg_ref, kvg_ref, w_kv_ref, out_ref,
                     kt_ref, v_ref, hn_ref, q_ref, sg_ref):
    step = pl.program_id(1)
    n_b = DEPTH - N_A
    heads = _HEAD_SLICES
    rows = [slice(t * TQ, (t + 1) * TQ) for t in range(TILES)]
    blocks = [step * TILES + t for t in range(TILES)]

    half = TK // 2
    row = lax.broadcasted_iota(jnp.int32, (TK, TK), 0)
    col = lax.broadcasted_iota(jnp.int32, (TK, TK), 1)
    upper = {TK: jnp.where(row > col, 1.0, 0.0).astype(_BF16)}
    upper[half] = upper[TK][:half, :half]
    causal_top = (col < row)[:half, :half]
    causal_bottom = (col < row)[half:, :]

    def sweep_steps(items, tots, results):
        zs, nls, excl = {}, {}, {}

        def logits(k):
            q, h, blk, kn, mask = items[k][:5]
            z = _dot(q(), kt_ref[blk, heads[h], :kn])
            zs[k] = z if mask is None else jnp.where(mask, z, -_HUGE_CARRY)

        def softplus(k):
            z = zs[k]
            nls[k] = jnp.maximum(z, 0.0) + jnp.log(1.0 + jnp.exp2(jnp.abs(z) * _NEG_LOG2E))

        def suffix(k):
            excl[k] = _dot(nls[k].astype(_BF16), upper[items[k][3]])
            tots[k] = excl[k][:, 0:1] + nls[k][:, 0:1]

        def weigh(k):
            _, h, blk, kn, _, carry = items[k]
            below = nls.pop(k) + excl.pop(k)
            if carry is not None:
                below = below + carry()
            a = jnp.exp(zs.pop(k) - below)
            k0 = pl.multiple_of(blk * TK, TK)
            results[k] = _dot(a.astype(_BF16), v_ref[pl.ds(k0, kn), heads[h]])

        order = range(len(items))
        return [lambda k=k, f=f: f(k) for f in (logits, softplus, suffix, weigh) for k in order]

    def live(carries):
        low = carries[0]
        for c in carries[1:]:
            low = jnp.minimum(low, c)
        return (jnp.min(low) < -_EXP_ZERO_CUTOFF).astype(jnp.int32)

    def window_steps(t, holder):
        blk = blocks[t]
        prev = jnp.maximum(blk - 1, 0)
        no_prev = jnp.where(blk > 0, 0.0, _HUGE_CARRY)
        d_items, d_tots, d_pvs, diag = [], {}, {}, {}
        for h, hs in enumerate(heads):
            d_items.append((lambda hs=hs: q_ref[t, :half, hs], h, blk, half, causal_top, None))
            d_items.append((lambda hs=hs: q_ref[t, half:, hs], h, blk, TK, causal_bottom, None))

        def diag_carry(h):
            if h not in diag:
                diag[h] = jnp.concatenate([d_tots[2 * h], d_tots[2 * h + 1]], axis=0) + no_prev
            return diag[h]

        p_items = [(lambda hs=hs: q_ref[t, :, hs], h, prev, TK, None,
                    lambda h=h: diag_carry(h)) for h, hs in enumerate(heads)]
        p_tots, p_pvs = {}, {}
        d_steps = sweep_steps(d_items, d_tots, d_pvs)
        p_steps = sweep_steps(p_items, p_tots, p_pvs)
        nd, npv = len(d_items), len(p_items)

        def carries():
            holder["carry"] = [diag_carry(h) + p_tots[h] for h in range(B_HEADS)]
            holder["go"] = live(holder["carry"])

        def finish():
            holder["acc"] = [
                jnp.concatenate([d_pvs[2 * h], d_pvs[2 * h + 1]], axis=0) + p_pvs[h]
                for h in range(B_HEADS)]

        ordered = []
        per = B_HEADS // HEAD_GROUPS
        for grp in range(HEAD_GROUPS):
            hs_ = range(grp * per, (grp + 1) * per)
            for stage in range(4):
                for h in hs_:
                    ordered += d_steps[stage * nd + 2 * h:stage * nd + 2 * h + 2]
                for h in hs_:
                    ordered.append(p_steps[stage * npv + h])
                if stage == 2 and grp == HEAD_GROUPS - 1:
                    ordered.append(carries)
        return ordered + [finish]

    def finish_sweep(t, holder):
        def full_block(j, carry, acc):
            tots, pvs = {}, {}
            items = [(lambda hs=hs: q_ref[t, :, hs], h, j, TK, None,
                      lambda h=h: carry[h]) for h, hs in enumerate(heads)]
            for f in sweep_steps(items, tots, pvs):
                f()
            return ([carry[h] + tots[h] for h in range(B_HEADS)],
                    [acc[h] + pvs[h] for h in range(B_HEADS)])

        def cond(loop):
            return jnp.logical_and(loop[0] >= 0, loop[1] > 0)

        def body(loop):
            carry, acc = full_block(loop[0], loop[2], loop[3])
            return loop[0] - 1, live(carry), carry, acc

        out = lax.while_loop(cond, body, (blocks[t] - 2, holder["go"], holder["carry"], holder["acc"]))
        holder["acc"] = out[3]

    def kv_steps(t):
        r0 = pl.multiple_of(blocks[t] * TQ, TQ)

        def norm():
            hn_ref[t] = _rms(out_ref[rows[t], :], kvg_ref[...]).astype(_BF16)

        def chunk(c):
            kv = _dot(hn_ref[t], w_kv_ref[:, c * COLS:(c + 1) * COLS])
            cs = _COL_SLICES[c % (D_MODEL // COLS)]
            if c < D_MODEL // COLS:
                kt_ref[blocks[t], cs, :] = kv.T.astype(_BF16)
            else:
                v_ref[pl.ds(r0, TQ), cs] = kv.astype(_BF16)

        return [norm] + [lambda c=c: chunk(c) for c in range(2 * D_MODEL // COLS)]

    def pre_steps(t, j):
        def norm():
            hn_ref[t] = _rms(out_ref[rows[t], :], ng_ref[j]).astype(_BF16)

        def query(c):
            cs = _COL_SLICES[c]
            q_ref[t, :, cs] = (_dot(hn_ref[t], w_in_ref[j, :, cs]) * _Q_SCALE).astype(_BF16)

        def gate(c):
            cs = _COL_SLICES[c]
            g = _dot(hn_ref[t], w_in_ref[j, :, D_MODEL + c * COLS:D_MODEL + (c + 1) * COLS])
            sg_ref[t, :, cs] = _silu(g).astype(_BF16)

        n_c = len(_COL_SLICES)
        return ([norm] + [lambda c=c: query(c) for c in range(n_c)]
                + [lambda c=c: gate(c) for c in range(n_c)])

    def post_steps(t, j, holder):
        def gated():
            for hh, hs in enumerate(heads):
                sg_ref[t, :, hs] = holder["acc"][hh].astype(_BF16) * sg_ref[t, :, hs]

        def mix(c):
            cs = _COL_SLICES[c]
            h1 = out_ref[rows[t], cs] + _dot(sg_ref[t], w_out_ref[j, :, cs])
            out_ref[rows[t], cs] = h1
            hn_ref[t, :, cs] = h1.astype(_BF16)

        def ple(c):
            cs = _COL_SLICES[c]
            pg = _dot(hn_ref[t], gate_w_ref[j, :, cs])
            pe = _dot(p_ref[j, rows[t], :].astype(_BF16), ple_w_ref[j, :, cs])
            out_ref[rows[t], cs] = out_ref[rows[t], cs] + jax.nn.sigmoid(pg) * pe

        n_c = len(_COL_SLICES)
        return ([gated] + [lambda c=c: mix(c) for c in range(n_c)]
                + [lambda c=c: ple(c) for c in range(n_c)])

    def program(sweep):
        x, y = 0, 1

        def settle(t, holder, more):
            if sweep:
                finish_sweep(t, holder)
                return more
            return jnp.maximum(more, jnp.where(blocks[t] >= 2, holder["go"], 0))

        def layer(j, more):
            att = [{} for _ in range(TILES)]
            for f in pre_steps(x, j):
                f()
            _interleave(window_steps(x, att[x]), pre_steps(y, j))
            more = settle(x, att[x], more)
            _interleave(window_steps(y, att[y]), post_steps(x, j, att[x]))
            more = settle(y, att[y], more)
            for f in post_steps(y, j, att[y]):
                f()
            return more

        out_ref[...] = h_ref[...]
        for f in kv_steps(x) + kv_steps(y):
            f()
        more = jnp.int32(0)
        for j in range(n_b):
            more = layer(j, more)
        for t in range(TILES):
            out_ref[rows[t], :] = _rms(out_ref[rows[t], :], fg_ref[...])
        return more

    redo = program(sweep=False)

    @pl.when(redo > 0)
    def _():
        program(sweep=True)


def _b_layers(h3, p2, norm_g, w_in, w_out, gate_w, ple_w, final_g, kv_g, w_kv):
    b, s, d = h3.shape
    tm = TILES * TQ
    n_steps = s // tm
    n_b = DEPTH - N_A

    def const(shape):
        nd = len(shape)
        return pl.BlockSpec(shape, lambda bb, ii, _nd=nd: (0,) * _nd,
                            pipeline_mode=pl.Buffered(1))

    def upper_half(shape):
        nd = len(shape)
        return pl.BlockSpec(shape, lambda bb, ii, _nd=nd: (N_A // n_b,) + (0,) * (_nd - 1),
                            pipeline_mode=pl.Buffered(1))

    tile = lambda: pl.BlockSpec((None, tm, d), lambda bb, ii: (bb, ii, 0))
    return pl.pallas_call(
        _b_layers_kernel,
        grid=(b, n_steps),
        in_specs=[
            tile(),
            pl.BlockSpec((n_b, tm, PLE_DIM), lambda bb, ii: (N_A // n_b, bb * n_steps + ii, 0)),
            upper_half((n_b, 1, d)),
            const((n_b, d, 2 * d)),
            const((n_b, d, d)),
            upper_half((n_b, d, d)),
            upper_half((n_b, PLE_DIM, d)),
            const((1, d)),
            const((1, d)),
            const((d, 2 * d)),
        ],
        out_specs=tile(),
        out_shape=jax.ShapeDtypeStruct((b, s, d), _F32),
        scratch_shapes=[
            pltpu.VMEM((s // TK, d, TK), _BF16),
            pltpu.VMEM((s, d), _BF16),
            pltpu.VMEM((TILES, TQ, d), _BF16),
            pltpu.VMEM((TILES, TQ, d), _BF16),
            pltpu.VMEM((TILES, TQ, d), _BF16),
        ],
        compiler_params=pltpu.CompilerParams(
            dimension_semantics=("arbitrary", "arbitrary"),
            vmem_limit_bytes=VMEM_LIMIT),
        name="b_layers",
    )(h3, p2, norm_g, w_in, w_out, gate_w, ple_w, final_g, kv_g, w_kv)


def kernel(x, p, norm_g, a_w_in, a_ln_g, a_ln_b, a_w_s, a_b_s, a_w_out, kv_norm_g, w_kv, b_w_in, b_w_out, ple_w, ple_gate_w, final_g):
    b, s, d = x.shape
    t = b * s
    bf = lambda w: w.astype(_BF16)

    h = x.reshape(t, d)
    p2 = p.reshape(DEPTH, t, PLE_DIM)
    norm_g3 = norm_g.reshape(DEPTH, 1, d)
    gate_w, ple_wb = bf(ple_gate_w), bf(ple_w)

    a_params = (norm_g3, bf(a_w_in), a_ln_g.reshape(N_A, 1, -1), a_ln_b.reshape(N_A, 1, -1),
                a_w_s, jnp.swapaxes(a_b_s, 1, 2), bf(a_w_out), gate_w, ple_wb)
    for i in range(N_A):
        h = _a_layer(h, p2, i, *a_params)

    return _b_layers(h.reshape(b, s, d), p2, norm_g3, bf(b_w_in), bf(b_w_out), gate_w, ple_wb,
                     final_g.reshape(1, d), kv_norm_g.reshape(1, d), bf(w_kv))
```

```python
import functools

import jax
import jax.numpy as jnp
import numpy as np
from jax import lax
from jax.experimental import pallas as pl
from jax.experimental.pallas import tpu as pltpu

D_MODEL = 1024
DEPTH = 4
N_A = DEPTH // 2
CHUNK = 64
GM_BLOCK = 128
A_WIDTH = 2 * D_MODEL
A_GROUPS = 8
A_GROUP_CH = A_WIDTH // A_GROUPS
B_HEADS = 8
B_HEAD_DIM = D_MODEL // B_HEADS
PLE_DIM = 256
EPS = 1e-6

COLS = 256
TM_A = 512
TQ = 256
TK = 256
TILES = 2
HEAD_GROUPS = 4
VMEM_LIMIT = 60 * 1024 * 1024

_SQRT_HALF = float(np.sqrt(0.5))
_Q_SCALE = float(1.0 / np.sqrt(B_HEAD_DIM))
_EXP_ZERO_CUTOFF = -105.0
_HUGE_CARRY = 1e30
_NEG_LOG2E = float(-1.0 / np.log(2.0))
_F32 = jnp.float32
_BF16 = jnp.bfloat16


def _rms(x, g):
    ms = jnp.mean(x * x, axis=-1, keepdims=True)
    return x * lax.rsqrt(ms + EPS) * g


def _gelu(x):
    return 0.5 * x * (1.0 + lax.erf(x * _SQRT_HALF))


def _silu(x):
    return x * jax.nn.sigmoid(x)


def _dot(a, b):
    return jnp.dot(a, b, preferred_element_type=_F32)


_HEAD_SLICES = [slice(hh * B_HEAD_DIM, (hh + 1) * B_HEAD_DIM) for hh in range(B_HEADS)]
_COL_SLICES = [slice(c * COLS, (c + 1) * COLS) for c in range(D_MODEL // COLS)]


def _interleave(primary, filler):
    done = 0
    for k, step in enumerate(primary):
        step()
        upto = (k + 1) * len(filler) // len(primary)
        for f in filler[done:upto]:
            f()
        done = upto


def _a_layer_kernel(n_cast, h_ref, p_ref, ng_ref, w_in_ref, lng_ref, lnb_ref, ws_ref,
                    bst_ref, w_out_ref, gate_w_ref, ple_w_ref, *refs):
    cast_in, out_ref, cast_out = refs[:n_cast], refs[n_cast], refs[n_cast + 1:2 * n_cast + 1]
    hn_ref, v_ref, vn_ref, ug_ref, y_ref, h1b_ref = refs[2 * n_cast + 1:]
    tm = h_ref.shape[0]
    for src, dst in zip(cast_in, cast_out):
        dst[...] = src[...].astype(_BF16)

    n_grp = tm // GM_BLOCK
    chunks = [slice(c * COLS, (c + 1) * COLS) for c in range(A_WIDTH // COLS)]

    hn_ref[...] = _rms(h_ref[...], ng_ref[...]).astype(_BF16)

    stat = {"s1": jnp.zeros((tm, 1), _F32)}
    for c, cs in enumerate(chunks):
        vc = _gelu(_dot(hn_ref[...], w_in_ref[:, A_WIDTH + c * COLS:A_WIDTH + (c + 1) * COLS]))
        v_ref[:, cs] = vc
        stat["s1"] = stat["s1"] + jnp.sum(vc, axis=-1, keepdims=True)

    def mean():
        stat["mu"] = stat["s1"] * (1.0 / A_WIDTH)
        stat["s2"] = jnp.zeros((tm, 1), _F32)

    def center(cs):
        d = v_ref[:, cs] - stat["mu"]
        stat["s2"] = stat["s2"] + jnp.sum(d * d, axis=-1, keepdims=True)

    def scale():
        stat["rstd"] = lax.rsqrt(stat["s2"] * (1.0 / A_WIDTH) + EPS)

    def normalize(cs):
        vn = (v_ref[:, cs] - stat["mu"]) * stat["rstd"] * lng_ref[:, cs] + lnb_ref[:, cs]
        vn_ref[:, cs] = vn.astype(_BF16)

    def u_gate(g):
        cs = slice(g * A_GROUP_CH, (g + 1) * A_GROUP_CH)
        u = _gelu(_dot(hn_ref[...], w_in_ref[:, cs]))
        gate = _dot(hn_ref[...], w_in_ref[:, 2 * A_WIDTH + g * A_GROUP_CH:2 * A_WIDTH + (g + 1) * A_GROUP_CH])
        ug_ref[:, cs] = u * _silu(gate)

    layernorm = ([mean] + [lambda cs=cs: center(cs) for cs in chunks] + [scale]
                 + [lambda cs=cs: normalize(cs) for cs in chunks])
    _interleave([lambda g=g: u_gate(g) for g in range(A_GROUPS)], layernorm)

    t_chunk = lax.broadcasted_iota(jnp.int32, (GM_BLOCK, GM_BLOCK), 0) // CHUNK
    s_chunk = lax.broadcasted_iota(jnp.int32, (GM_BLOCK, GM_BLOCK), 1) // CHUNK
    mask = s_chunk <= t_chunk

    for g in range(A_GROUPS):
        cs = slice(g * A_GROUP_CH, (g + 1) * A_GROUP_CH)
        wm = jnp.where(mask, ws_ref[g], 0.0).astype(_BF16)
        b_col = bst_ref[:, g:g + 1]
        sv = jnp.concatenate(
            [_dot(wm, vn_ref[n * GM_BLOCK:(n + 1) * GM_BLOCK, cs]) + b_col
             for n in range(n_grp)], axis=0)
        y_ref[:, cs] = (ug_ref[:, cs] * sv).astype(_BF16)

    for cs in _COL_SLICES:
        h1 = h_ref[:, cs] + _dot(y_ref[...], w_out_ref[:, cs])
        out_ref[:, cs] = h1
        h1b_ref[:, cs] = h1.astype(_BF16)
    pb = p_ref[...].astype(_BF16)
    for cs in _COL_SLICES:
        pg = _dot(h1b_ref[...], gate_w_ref[:, cs])
        pe = _dot(pb, ple_w_ref[:, cs])
        out_ref[:, cs] = out_ref[:, cs] + jax.nn.sigmoid(pg) * pe


def _a_layer(h, p2, layer, norm_g, ln_g, ln_b, w_s, b_s_t, ple_w, w_in, w_out, gate_w, gate_layer, casts=()):
    t = h.shape[0]
    tm = TM_A
    n_steps = t // tm

    def param(index, *shape):
        return pl.BlockSpec((None,) + shape, lambda i: (index,) + (0,) * len(shape),
                            pipeline_mode=pl.Buffered(1))

    cast_in, cast_out, cast_shapes = [], [], []
    for src, slab, first in casts:
        cols = src.shape[1]
        cast_in.append(pl.BlockSpec((slab, cols), lambda i, _f=first: (_f + i, 0)))
        cast_out.append(pl.BlockSpec((slab, cols), lambda i: (i, 0)))
        cast_shapes.append(jax.ShapeDtypeStruct((slab * n_steps, cols), _BF16))

    out = pl.pallas_call(
        functools.partial(_a_layer_kernel, len(casts)),
        grid=(n_steps,),
        in_specs=[
            pl.BlockSpec((tm, D_MODEL), lambda i: (i, 0)),
            pl.BlockSpec((None, tm, PLE_DIM), lambda i: (layer, i, 0)),
            param(layer, 1, D_MODEL),
            param(0, D_MODEL, 3 * A_WIDTH),
            param(layer, 1, A_WIDTH),
            param(layer, 1, A_WIDTH),
            param(layer, A_GROUPS, GM_BLOCK, GM_BLOCK),
            param(layer, GM_BLOCK, A_GROUPS),
            param(0, A_WIDTH, D_MODEL),
            param(gate_layer, D_MODEL, D_MODEL),
            param(layer, PLE_DIM, D_MODEL),
        ] + cast_in,
        out_specs=[pl.BlockSpec((tm, D_MODEL), lambda i: (i, 0))] + cast_out,
        out_shape=[jax.ShapeDtypeStruct((t, D_MODEL), _F32)] + cast_shapes,
        scratch_shapes=[
            pltpu.VMEM((tm, D_MODEL), _BF16),
            pltpu.VMEM((tm, A_WIDTH), _F32),
            pltpu.VMEM((tm, A_WIDTH), _BF16),
            pltpu.VMEM((tm, A_WIDTH), _F32),
            pltpu.VMEM((tm, A_WIDTH), _BF16),
            pltpu.VMEM((tm, D_MODEL), _BF16),
        ],
        compiler_params=pltpu.CompilerParams(
            dimension_semantics=("arbitrary",), vmem_limit_bytes=VMEM_LIMIT),
        name="a_layer",
    )(h, p2, norm_g, w_in, ln_g, ln_b, w_s, b_s_t, w_out, gate_w, ple_w, *[c[0] for c in casts])
    return out[0], out[1:]


def _b_layers_kernel(h_ref, p_ref, ng_ref, w_in_ref, w_out_ref, gate_w_ref,
                     ple_w_ref, fg_ref, kvg_ref, w_kv_ref, out_ref,
                     kt_ref, v_ref, hn_ref, q_ref, sg_ref):
    step = pl.program_id(1)
    n_b = DEPTH - N_A
    heads = _HEAD_SLICES
    rows = [slice(t * TQ, (t + 1) * TQ) for t in range(TILES)]
    blocks = [step * TILES + t for t in range(TILES)]

    half = TK // 2
    row = lax.broadcasted_iota(jnp.int32, (TK, TK), 0)
    col = lax.broadcasted_iota(jnp.int32, (TK, TK), 1)
    upper = {TK: jnp.where(row > col, 1.0, 0.0).astype(_BF16)}
    upper[half] = upper[TK][:half, :half]
    causal_top = (col < row)[:half, :half]
    causal_bottom = (col < row)[half:, :]

    def sweep_steps(items, tots, results):
        zs, nls, excl = {}, {}, {}

        def logits(k):
            q, h, blk, kn, mask = items[k][:5]
            z = _dot(q(), kt_ref[blk, heads[h], :kn])
            zs[k] = z if mask is None else jnp.where(mask, z, -_HUGE_CARRY)

        def softplus(k):
            z = zs[k]
            nls[k] = jnp.maximum(z, 0.0) + jnp.log(1.0 + jnp.exp2(jnp.abs(z) * _NEG_LOG2E))

        def suffix(k):
            excl[k] = _dot(nls[k].astype(_BF16), upper[items[k][3]])
            tots[k] = excl[k][:, 0:1] + nls[k][:, 0:1]

        def weigh(k):
            _, h, blk, kn, _, carry = items[k]
            below = nls.pop(k) + excl.pop(k)
            if carry is not None:
                below = below + carry()
            a = jnp.exp(zs.pop(k) - below)
            k0 = pl.multiple_of(blk * TK, TK)
            results[k] = _dot(a.astype(_BF16), v_ref[pl.ds(k0, kn), heads[h]])

        order = range(len(items))
        return [lambda k=k, f=f: f(k) for f in (logits, softplus, suffix, weigh) for k in order]

    def live(carries):
        low = carries[0]
        for c in carries[1:]:
            low = jnp.minimum(low, c)
        return (jnp.min(low) < -_EXP_ZERO_CUTOFF).astype(jnp.int32)

    def window_steps(t, holder):
        blk = blocks[t]
        prev = jnp.maximum(blk - 1, 0)
        no_prev = jnp.where(blk > 0, 0.0, _HUGE_CARRY)
        d_items, d_tots, d_pvs, diag = [], {}, {}, {}
        for h, hs in enumerate(heads):
            d_items.append((lambda hs=hs: q_ref[t, :half, hs], h, blk, half, causal_top, None))
            d_items.append((lambda hs=hs: q_ref[t, half:, hs], h, blk, TK, causal_bottom, None))

        def diag_carry(h):
            if h not in diag:
                diag[h] = jnp.concatenate([d_tots[2 * h], d_tots[2 * h + 1]], axis=0) + no_prev
            return diag[h]

        p_items = [(lambda hs=hs: q_ref[t, :, hs], h, prev, TK, None,
                    lambda h=h: diag_carry(h)) for h, hs in enumerate(heads)]
        p_tots, p_pvs = {}, {}
        d_steps = sweep_steps(d_items, d_tots, d_pvs)
        p_steps = sweep_steps(p_items, p_tots, p_pvs)
        nd, npv = len(d_items), len(p_items)

        def carries():
            holder["carry"] = [diag_carry(h) + p_tots[h] for h in range(B_HEADS)]
            holder["go"] = live(holder["carry"])

        def finish():
            holder["acc"] = [
                jnp.concatenate([d_pvs[2 * h], d_pvs[2 * h + 1]], axis=0) + p_pvs[h]
                for h in range(B_HEADS)]

        ordered = []
        per = B_HEADS // HEAD_GROUPS
        for grp in range(HEAD_GROUPS):
            hs_ = range(grp * per, (grp + 1) * per)
            for stage in range(4):
                for h in hs_:
                    ordered += d_steps[stage * nd + 2 * h:stage * nd + 2 * h + 2]
                for h in hs_:
                    ordered.append(p_steps[stage * npv + h])
                if stage == 2 and grp == HEAD_GROUPS - 1:
                    ordered.append(carries)
        return ordered + [finish]

    def finish_sweep(t, holder):
        def full_block(j, carry, acc):
            tots, pvs = {}, {}
            items = [(lambda hs=hs: q_ref[t, :, hs], h, j, TK, None,
                      lambda h=h: carry[h]) for h, hs in enumerate(heads)]
            for f in sweep_steps(items, tots, pvs):
                f()
            return ([carry[h] + tots[h] for h in range(B_HEADS)],
                    [acc[h] + pvs[h] for h in range(B_HEADS)])

        def cond(loop):
            return jnp.logical_and(loop[0] >= 0, loop[1] > 0)

        def body(loop):
            carry, acc = full_block(loop[0], loop[2], loop[3])
            return loop[0] - 1, live(carry), carry, acc

        out = lax.while_loop(cond, body, (blocks[t] - 2, holder["go"], holder["carry"], holder["acc"]))
        holder["acc"] = out[3]

    def kv_steps(t):
        r0 = pl.multiple_of(blocks[t] * TQ, TQ)

        def norm():
            hn_ref[t] = _rms(out_ref[rows[t], :], kvg_ref[...]).astype(_BF16)

        def chunk(c):
            kv = _dot(hn_ref[t], w_kv_ref[:, c * COLS:(c + 1) * COLS])
            cs = _COL_SLICES[c % (D_MODEL // COLS)]
            if c < D_MODEL // COLS:
                kt_ref[blocks[t], cs, :] = kv.T.astype(_BF16)
            else:
                v_ref[pl.ds(r0, TQ), cs] = kv.astype(_BF16)

        return [norm] + [lambda c=c: chunk(c) for c in range(2 * D_MODEL // COLS)]

    def pre_steps(t, j):
        def norm():
            hn_ref[t] = _rms(out_ref[rows[t], :], ng_ref[j]).astype(_BF16)

        def query(c):
            cs = _COL_SLICES[c]
            q_ref[t, :, cs] = (_dot(hn_ref[t], w_in_ref[j, :, cs]) * _Q_SCALE).astype(_BF16)

        def gate(c):
            cs = _COL_SLICES[c]
            g = _dot(hn_ref[t], w_in_ref[j, :, D_MODEL + c * COLS:D_MODEL + (c + 1) * COLS])
            sg_ref[t, :, cs] = _silu(g).astype(_BF16)

        n_c = len(_COL_SLICES)
        return ([norm] + [lambda c=c: query(c) for c in range(n_c)]
                + [lambda c=c: gate(c) for c in range(n_c)])

    def post_steps(t, j, holder):
        def gated():
            for hh, hs in enumerate(heads):
                sg_ref[t, :, hs] = holder["acc"][hh].astype(_BF16) * sg_ref[t, :, hs]

        def mix(c):
            cs = _COL_SLICES[c]
            h1 = out_ref[rows[t], cs] + _dot(sg_ref[t], w_out_ref[j, :, cs])
            out_ref[rows[t], cs] = h1
            hn_ref[t, :, cs] = h1.astype(_BF16)

        def ple(c):
            cs = _COL_SLICES[c]
            pg = _dot(hn_ref[t], gate_w_ref[j, :, cs])
            pe = _dot(p_ref[j, rows[t], :].astype(_BF16), ple_w_ref[j, :, cs])
            out_ref[rows[t], cs] = out_ref[rows[t], cs] + jax.nn.sigmoid(pg) * pe

        n_c = len(_COL_SLICES)
        return ([gated] + [lambda c=c: mix(c) for c in range(n_c)]
                + [lambda c=c: ple(c) for c in range(n_c)])

    def program(sweep):
        x, y = 0, 1

        def settle(t, holder, more):
            if sweep:
                finish_sweep(t, holder)
                return more
            return jnp.maximum(more, jnp.where(blocks[t] >= 2, holder["go"], 0))

        def layer(j, more):
            att = [{} for _ in range(TILES)]
            for f in pre_steps(x, j):
                f()
            _interleave(window_steps(x, att[x]), pre_steps(y, j))
            more = settle(x, att[x], more)
            _interleave(window_steps(y, att[y]), post_steps(x, j, att[x]))
            more = settle(y, att[y], more)
            for f in post_steps(y, j, att[y]):
                f()
            return more

        out_ref[...] = h_ref[...]
        for f in kv_steps(x) + kv_steps(y):
            f()
        more = jnp.int32(0)
        for j in range(n_b):
            more = layer(j, more)
        for t in range(TILES):
            out_ref[rows[t], :] = _rms(out_ref[rows[t], :], fg_ref[...])
        return more

    redo = program(sweep=False)

    @pl.when(redo > 0)
    def _():
        program(sweep=True)


def _b_layers(h3, p2, norm_g, w_in, w_out, gate_w, ple_w, final_g, kv_g, w_kv):
    b, s, d = h3.shape
    tm = TILES * TQ
    n_steps = s // tm
    n_b = DEPTH - N_A

    def const(shape):
        nd = len(shape)
        return pl.BlockSpec(shape, lambda bb, ii, _nd=nd: (0,) * _nd,
                            pipeline_mode=pl.Buffered(1))

    def upper_half(shape):
        nd = len(shape)
        return pl.BlockSpec(shape, lambda bb, ii, _nd=nd: (N_A // n_b,) + (0,) * (_nd - 1),
                            pipeline_mode=pl.Buffered(1))

    tile = lambda: pl.BlockSpec((None, tm, d), lambda bb, ii: (bb, ii, 0))
    return pl.pallas_call(
        _b_layers_kernel,
        grid=(b, n_steps),
        in_specs=[
            tile(),
            pl.BlockSpec((n_b, tm, PLE_DIM), lambda bb, ii: (N_A // n_b, bb * n_steps + ii, 0)),
            upper_half((n_b, 1, d)),
            const((n_b, d, 2 * d)),
            const((n_b, d, d)),
            upper_half((n_b, d, d)),
            upper_half((n_b, PLE_DIM, d)),
            const((1, d)),
            const((1, d)),
            const((d, 2 * d)),
        ],
        out_specs=tile(),
        out_shape=jax.ShapeDtypeStruct((b, s, d), _F32),
        scratch_shapes=[
            pltpu.VMEM((s // TK, d, TK), _BF16),
            pltpu.VMEM((s, d), _BF16),
            pltpu.VMEM((TILES, TQ, d), _BF16),
            pltpu.VMEM((TILES, TQ, d), _BF16),
            pltpu.VMEM((TILES, TQ, d), _BF16),
        ],
        compiler_params=pltpu.CompilerParams(
            dimension_semantics=("arbitrary", "arbitrary"),
            vmem_limit_bytes=VMEM_LIMIT),
        name="b_layers",
    )(h3, p2, norm_g, w_in, w_out, gate_w, ple_w, final_g, kv_g, w_kv)


def kernel(x, p, norm_g, a_w_in, a_ln_g, a_ln_b, a_w_s, a_b_s, a_w_out, kv_norm_g, w_kv, b_w_in, b_w_out, ple_w, ple_gate_w, final_g):
    b, s, d = x.shape
    t = b * s
    n_b = DEPTH - N_A
    bf = lambda w: w.astype(_BF16)

    h = x.reshape(t, d)
    p2 = p.reshape(DEPTH, t, PLE_DIM)
    norm_g3 = norm_g.reshape(DEPTH, 1, d)
    ple_wb = bf(ple_w)
    small = (norm_g3, a_ln_g.reshape(N_A, 1, -1), a_ln_b.reshape(N_A, 1, -1), a_w_s,
             jnp.swapaxes(a_b_s, 1, 2), ple_wb)

    n_steps = t // TM_A
    slab = lambda rows: rows // n_steps
    casts = (
        (a_w_in.reshape(N_A * d, -1), slab(d), n_steps),
        (a_w_out.reshape(N_A * A_WIDTH, d), slab(A_WIDTH), n_steps),
        (b_w_in.reshape(n_b * d, -1), slab(n_b * d), 0),
        (b_w_out.reshape(n_b * d, d), slab(n_b * d), 0),
        (w_kv, slab(d), 0),
        (ple_gate_w.reshape(DEPTH * d, d), slab(DEPTH * d), 0),
    )
    h, (w_in1, w_out1, b_w_in_b, b_w_out_b, w_kv_b, gate_b) = _a_layer(
        h, p2, 0, *small, bf(a_w_in[0])[None], bf(a_w_out[0])[None], bf(ple_gate_w[0])[None], 0,
        casts=casts)
    gate_b = gate_b.reshape(DEPTH, d, d)
    for i in range(1, N_A):
        h, _ = _a_layer(h, p2, i, *small, w_in1[None], w_out1[None], gate_b, i)

    return _b_layers(h.reshape(b, s, d), p2, norm_g3, b_w_in_b.reshape(n_b, d, -1),
                     b_w_out_b.reshape(n_b, d, d), gate_b, ple_wb,
                     final_g.reshape(1, d), kv_norm_g.reshape(1, d), w_kv_b)
```

```python
import functools

import jax
import jax.numpy as jnp
import numpy as np
from jax import lax
from jax.experimental import pallas as pl
from jax.experimental.pallas import tpu as pltpu

D_MODEL = 1024
DEPTH = 4
N_A = DEPTH // 2
CHUNK = 64
GM_BLOCK = 128
A_WIDTH = 2 * D_MODEL
A_GROUPS = 8
A_GROUP_CH = A_WIDTH // A_GROUPS
B_HEADS = 8
B_HEAD_DIM = D_MODEL // B_HEADS
PLE_DIM = 256
EPS = 1e-6

COLS = 256
TM_A = 512
TQ = 256
TK = 256
TILES = 2
HEAD_GROUPS = 4
VMEM_LIMIT = 60 * 1024 * 1024

_SQRT_HALF = float(np.sqrt(0.5))
_Q_SCALE = float(1.0 / np.sqrt(B_HEAD_DIM))
_EXP_ZERO_CUTOFF = -105.0
_HUGE_CARRY = 1e30
_NEG_LOG2E = float(-1.0 / np.log(2.0))
_F32 = jnp.float32
_BF16 = jnp.bfloat16


def _rms(x, g):
    ms = jnp.mean(x * x, axis=-1, keepdims=True)
    return x * lax.rsqrt(ms + EPS) * g


def _gelu(x):
    return 0.5 * x * (1.0 + lax.erf(x * _SQRT_HALF))


def _silu(x):
    return x * jax.nn.sigmoid(x)


def _dot(a, b):
    return jnp.dot(a, b, preferred_element_type=_F32)


_HEAD_SLICES = [slice(hh * B_HEAD_DIM, (hh + 1) * B_HEAD_DIM) for hh in range(B_HEADS)]
_COL_SLICES = [slice(c * COLS, (c + 1) * COLS) for c in range(D_MODEL // COLS)]


def _interleave(primary, filler):
    done = 0
    for k, step in enumerate(primary):
        step()
        upto = (k + 1) * len(filler) // len(primary)
        for f in filler[done:upto]:
            f()
        done = upto


def _a_layer_kernel(n_cast, h_ref, h_next_ref, p_ref, ng_ref, w_in_ref, lng_ref, lnb_ref, ws_ref,
                    bst_ref, w_out_ref, gate_w_ref, ple_w_ref, *refs):
    cast_in, out_ref, cast_out = refs[:n_cast], refs[n_cast], refs[n_cast + 1:2 * n_cast + 1]
    hn2_ref, v_ref, vn_ref, ug_ref, y_ref, h1b_ref = refs[2 * n_cast + 1:]
    tm = h_ref.shape[0]
    for src, dst in zip(cast_in, cast_out):
        dst[...] = src[...].astype(_BF16)

    n_grp = tm // GM_BLOCK
    chunks = [slice(c * COLS, (c + 1) * COLS) for c in range(A_WIDTH // COLS)]

    step = pl.program_id(0)
    cur = step % 2
    hn_ref = hn2_ref.at[cur]

    @pl.when(step == 0)
    def _():
        hn2_ref[0] = _rms(h_ref[...], ng_ref[...]).astype(_BF16)

    stat = {"s1": jnp.zeros((tm, 1), _F32)}
    for c, cs in enumerate(chunks):
        vc = _gelu(_dot(hn_ref[...], w_in_ref[:, A_WIDTH + c * COLS:A_WIDTH + (c + 1) * COLS]))
        v_ref[:, cs] = vc
        stat["s1"] = stat["s1"] + jnp.sum(vc, axis=-1, keepdims=True)
        if c == 0:
            hn2_ref[1 - cur] = _rms(h_next_ref[...], ng_ref[...]).astype(_BF16)

    def mean():
        stat["mu"] = stat["s1"] * (1.0 / A_WIDTH)
        stat["s2"] = jnp.zeros((tm, 1), _F32)

    def center(cs):
        d = v_ref[:, cs] - stat["mu"]
        stat["s2"] = stat["s2"] + jnp.sum(d * d, axis=-1, keepdims=True)

    def scale():
        stat["rstd"] = lax.rsqrt(stat["s2"] * (1.0 / A_WIDTH) + EPS)

    def normalize(cs):
        vn = (v_ref[:, cs] - stat["mu"]) * stat["rstd"] * lng_ref[:, cs] + lnb_ref[:, cs]
        vn_ref[:, cs] = vn.astype(_BF16)

    def u_gate(g):
        cs = slice(g * A_GROUP_CH, (g + 1) * A_GROUP_CH)
        u = _gelu(_dot(hn_ref[...], w_in_ref[:, cs]))
        gate = _dot(hn_ref[...], w_in_ref[:, 2 * A_WIDTH + g * A_GROUP_CH:2 * A_WIDTH + (g + 1) * A_GROUP_CH])
        ug_ref[:, cs] = u * _silu(gate)

    layernorm = ([mean] + [lambda cs=cs: center(cs) for cs in chunks] + [scale]
                 + [lambda cs=cs: normalize(cs) for cs in chunks])
    _interleave([lambda g=g: u_gate(g) for g in range(A_GROUPS)], layernorm)

    t_chunk = lax.broadcasted_iota(jnp.int32, (GM_BLOCK, GM_BLOCK), 0) // CHUNK
    s_chunk = lax.broadcasted_iota(jnp.int32, (GM_BLOCK, GM_BLOCK), 1) // CHUNK
    mask = s_chunk <= t_chunk

    for g in range(A_GROUPS):
        cs = slice(g * A_GROUP_CH, (g + 1) * A_GROUP_CH)
        wm = jnp.where(mask, ws_ref[g], 0.0).astype(_BF16)
        b_col = bst_ref[:, g:g + 1]
        sv = jnp.concatenate(
            [_dot(wm, vn_ref[n * GM_BLOCK:(n + 1) * GM_BLOCK, cs]) + b_col
             for n in range(n_grp)], axis=0)
        y_ref[:, cs] = (ug_ref[:, cs] * sv).astype(_BF16)

    for cs in _COL_SLICES:
        h1 = h_ref[:, cs] + _dot(y_ref[...], w_out_ref[:, cs])
        out_ref[:, cs] = h1
        h1b_ref[:, cs] = h1.astype(_BF16)
    pb = p_ref[...].astype(_BF16)
    for cs in _COL_SLICES:
        pg = _dot(h1b_ref[...], gate_w_ref[:, cs])
        pe = _dot(pb, ple_w_ref[:, cs])
        out_ref[:, cs] = out_ref[:, cs] + jax.nn.sigmoid(pg) * pe


def _a_layer(h, p2, layer, norm_g, ln_g, ln_b, w_s, b_s_t, ple_w, w_in, w_out, gate_w, gate_layer, casts=()):
    t = h.shape[0]
    tm = TM_A
    n_steps = t // tm

    def param(index, *shape):
        return pl.BlockSpec((None,) + shape, lambda i: (index,) + (0,) * len(shape),
                            pipeline_mode=pl.Buffered(1))

    cast_in, cast_out, cast_shapes = [], [], []
    for src, slab, first in casts:
        cols = src.shape[1]
        cast_in.append(pl.BlockSpec((slab, cols), lambda i, _f=first: (_f + i, 0)))
        cast_out.append(pl.BlockSpec((slab, cols), lambda i: (i, 0)))
        cast_shapes.append(jax.ShapeDtypeStruct((slab * n_steps, cols), _BF16))

    out = pl.pallas_call(
        functools.partial(_a_layer_kernel, len(casts)),
        grid=(n_steps,),
        in_specs=[
            pl.BlockSpec((tm, D_MODEL), lambda i: (i, 0)),
            pl.BlockSpec((tm, D_MODEL), lambda i: (jnp.minimum(i + 1, n_steps - 1), 0)),
            pl.BlockSpec((None, tm, PLE_DIM), lambda i: (layer, i, 0)),
            param(layer, 1, D_MODEL),
            param(0, D_MODEL, 3 * A_WIDTH),
            param(layer, 1, A_WIDTH),
            param(layer, 1, A_WIDTH),
            param(layer, A_GROUPS, GM_BLOCK, GM_BLOCK),
            param(layer, GM_BLOCK, A_GROUPS),
            param(0, A_WIDTH, D_MODEL),
            param(gate_layer, D_MODEL, D_MODEL),
            param(layer, PLE_DIM, D_MODEL),
        ] + cast_in,
        out_specs=[pl.BlockSpec((tm, D_MODEL), lambda i: (i, 0))] + cast_out,
        out_shape=[jax.ShapeDtypeStruct((t, D_MODEL), _F32)] + cast_shapes,
        scratch_shapes=[
            pltpu.VMEM((2, tm, D_MODEL), _BF16),
            pltpu.VMEM((tm, A_WIDTH), _F32),
            pltpu.VMEM((tm, A_WIDTH), _BF16),
            pltpu.VMEM((tm, A_WIDTH), _F32),
            pltpu.VMEM((tm, A_WIDTH), _BF16),
            pltpu.VMEM((tm, D_MODEL), _BF16),
        ],
        compiler_params=pltpu.CompilerParams(
            dimension_semantics=("arbitrary",), vmem_limit_bytes=VMEM_LIMIT),
        name="a_layer",
    )(h, h, p2, norm_g, w_in, ln_g, ln_b, w_s, b_s_t, w_out, gate_w, ple_w, *[c[0] for c in casts])
    return out[0], out[1:]


def _b_layers_kernel(h_ref, p_ref, ng_ref, w_in_ref, w_out_ref, gate_w_ref,
                     ple_w_ref, fg_ref, kvg_ref, w_kv_ref, out_ref,
                     kt_ref, v_ref, hn_ref, q_ref, sg_ref):
    step = pl.program_id(1)
    n_b = DEPTH - N_A
    heads = _HEAD_SLICES
    rows = [slice(t * TQ, (t + 1) * TQ) for t in range(TILES)]
    blocks = [step * TILES + t for t in range(TILES)]

    half = TK // 2
    row = lax.broadcasted_iota(jnp.int32, (TK, TK), 0)
    col = lax.broadcasted_iota(jnp.int32, (TK, TK), 1)
    upper = {TK: jnp.where(row > col, 1.0, 0.0).astype(_BF16)}
    upper[half] = upper[TK][:half, :half]
    causal_top = (col < row)[:half, :half]
    causal_bottom = (col < row)[half:, :]

    def sweep_steps(items, tots, results):
        zs, nls, excl = {}, {}, {}

        def logits(k):
            q, h, blk, kn, mask = items[k][:5]
            z = _dot(q(), kt_ref[blk, heads[h], :kn])
            zs[k] = z if mask is None else jnp.where(mask, z, -_HUGE_CARRY)

        def softplus(k):
            z = zs[k]
            nls[k] = jnp.maximum(z, 0.0) + jnp.log(1.0 + jnp.exp2(jnp.abs(z) * _NEG_LOG2E))

        def suffix(k):
            excl[k] = _dot(nls[k].astype(_BF16), upper[items[k][3]])
            tots[k] = excl[k][:, 0:1] + nls[k][:, 0:1]

        def weigh(k):
            _, h, blk, kn, _, carry = items[k]
            below = nls.pop(k) + excl.pop(k)
            if carry is not None:
                below = below + carry()
            a = jnp.exp(zs.pop(k) - below)
            k0 = pl.multiple_of(blk * TK, TK)
            results[k] = _dot(a.astype(_BF16), v_ref[pl.ds(k0, kn), heads[h]])

        order = range(len(items))
        return [lambda k=k, f=f: f(k) for f in (logits, softplus, suffix, weigh) for k in order]

    def live(carries):
        low = carries[0]
        for c in carries[1:]:
            low = jnp.minimum(low, c)
        return (jnp.min(low) < -_EXP_ZERO_CUTOFF).astype(jnp.int32)

    def window_steps(t, holder):
        blk = blocks[t]
        prev = jnp.maximum(blk - 1, 0)
        no_prev = jnp.where(blk > 0, 0.0, _HUGE_CARRY)
        d_items, d_tots, d_pvs, diag = [], {}, {}, {}
        for h, hs in enumerate(heads):
            d_items.append((lambda hs=hs: q_ref[t, :half, hs], h, blk, half, causal_top, None))
            d_items.append((lambda hs=hs: q_ref[t, half:, hs], h, blk, TK, causal_bottom, None))

        def diag_carry(h):
            if h not in diag:
                diag[h] = jnp.concatenate([d_tots[2 * h], d_tots[2 * h + 1]], axis=0) + no_prev
            return diag[h]

        p_items = [(lambda hs=hs: q_ref[t, :, hs], h, prev, TK, None,
                    lambda h=h: diag_carry(h)) for h, hs in enumerate(heads)]
        p_tots, p_pvs = {}, {}
        d_steps = sweep_steps(d_items, d_tots, d_pvs)
        p_steps = sweep_steps(p_items, p_tots, p_pvs)
        nd, npv = len(d_items), len(p_items)

        def carries():
            holder["carry"] = [diag_carry(h) + p_tots[h] for h in range(B_HEADS)]
            holder["go"] = live(holder["carry"])

        def finish():
            holder["acc"] = [
                jnp.concatenate([d_pvs[2 * h], d_pvs[2 * h + 1]], axis=0) + p_pvs[h]
                for h in range(B_HEADS)]

        ordered = []
        per = B_HEADS // HEAD_GROUPS
        for grp in range(HEAD_GROUPS):
            hs_ = range(grp * per, (grp + 1) * per)
            for stage in range(4):
                for h in hs_:
                    ordered += d_steps[stage * nd + 2 * h:stage * nd + 2 * h + 2]
                for h in hs_:
                    ordered.append(p_steps[stage * npv + h])
                if stage == 2 and grp == HEAD_GROUPS - 1:
                    ordered.append(carries)
        return ordered + [finish]

    def finish_sweep(t, holder):
        def full_block(j, carry, acc):
            tots, pvs = {}, {}
            items = [(lambda hs=hs: q_ref[t, :, hs], h, j, TK, None,
                      lambda h=h: carry[h]) for h, hs in enumerate(heads)]
            for f in sweep_steps(items, tots, pvs):
                f()
            return ([carry[h] + tots[h] for h in range(B_HEADS)],
                    [acc[h] + pvs[h] for h in range(B_HEADS)])

        def cond(loop):
            return jnp.logical_and(loop[0] >= 0, loop[1] > 0)

        def body(loop):
            carry, acc = full_block(loop[0], loop[2], loop[3])
            return loop[0] - 1, live(carry), carry, acc

        out = lax.while_loop(cond, body, (blocks[t] - 2, holder["go"], holder["carry"], holder["acc"]))
        holder["acc"] = out[3]

    def kv_steps():
        r0 = pl.multiple_of(blocks[0] * TQ, TILES * TQ)

        def norm():
            for t in range(TILES):
                hn_ref[t] = _rms(h_ref[rows[t], :], kvg_ref[...]).astype(_BF16)

        def chunk(c):
            kv = _dot(hn_ref[...].reshape(TILES * TQ, D_MODEL), w_kv_ref[:, c * COLS:(c + 1) * COLS])
            cs = _COL_SLICES[c % (D_MODEL // COLS)]
            if c < D_MODEL // COLS:
                for t in range(TILES):
                    kt_ref[blocks[t], cs, :] = kv[rows[t]].T.astype(_BF16)
            else:
                v_ref[pl.ds(r0, TILES * TQ), cs] = kv.astype(_BF16)

        return [norm] + [lambda c=c: chunk(c) for c in range(2 * D_MODEL // COLS)]

    def stream(j):
        return h_ref if j == 0 else out_ref

    def pre_steps(t, j):
        def norm():
            hn_ref[t] = _rms(stream(j)[rows[t], :], ng_ref[j]).astype(_BF16)

        def query(c):
            cs = _COL_SLICES[c]
            q_ref[t, :, cs] = (_dot(hn_ref[t], w_in_ref[j, :, cs]) * _Q_SCALE).astype(_BF16)

        def gate(c):
            cs = _COL_SLICES[c]
            g = _dot(hn_ref[t], w_in_ref[j, :, D_MODEL + c * COLS:D_MODEL + (c + 1) * COLS])
            sg_ref[t, :, cs] = _silu(g).astype(_BF16)

        n_c = len(_COL_SLICES)
        return ([norm] + [lambda c=c: query(c) for c in range(n_c)]
                + [lambda c=c: gate(c) for c in range(n_c)])

    def post_steps(t, j, holder):
        def gated():
            for hh, hs in enumerate(heads):
                sg_ref[t, :, hs] = holder["acc"][hh].astype(_BF16) * sg_ref[t, :, hs]

        def mix(c):
            cs = _COL_SLICES[c]
            h1 = stream(j)[rows[t], cs] + _dot(sg_ref[t], w_out_ref[j, :, cs])
            out_ref[rows[t], cs] = h1
            hn_ref[t, :, cs] = h1.astype(_BF16)

        def ple(c):
            cs = _COL_SLICES[c]
            pg = _dot(hn_ref[t], gate_w_ref[j, :, cs])
            pe = _dot(p_ref[j, rows[t], :].astype(_BF16), ple_w_ref[j, :, cs])
            out_ref[rows[t], cs] = out_ref[rows[t], cs] + jax.nn.sigmoid(pg) * pe

        n_c = len(_COL_SLICES)
        return ([gated] + [lambda c=c: mix(c) for c in range(n_c)]
                + [lambda c=c: ple(c) for c in range(n_c)])

    def program(sweep):
        x, y = 0, 1

        def settle(t, holder, more):
            if sweep:
                finish_sweep(t, holder)
                return more
            return jnp.maximum(more, jnp.where(blocks[t] >= 2, holder["go"], 0))

        def layer(j, more):
            att = [{} for _ in range(TILES)]
            for f in pre_steps(x, j):
                f()
            _interleave(window_steps(x, att[x]), pre_steps(y, j))
            more = settle(x, att[x], more)
            _interleave(window_steps(y, att[y]), post_steps(x, j, att[x]))
            more = settle(y, att[y], more)
            for f in post_steps(y, j, att[y]):
                f()
            return more

        for f in kv_steps():
            f()
        more = jnp.int32(0)
        for j in range(n_b):
            more = layer(j, more)
        for t in range(TILES):
            out_ref[rows[t], :] = _rms(out_ref[rows[t], :], fg_ref[...])
        return more

    redo = program(sweep=False)

    @pl.when(redo > 0)
    def _():
        program(sweep=True)


def _b_layers(h3, p2, norm_g, w_in, w_out, gate_w, ple_w, final_g, kv_g, w_kv):
    b, s, d = h3.shape
    tm = TILES * TQ
    n_steps = s // tm
    n_b = DEPTH - N_A

    def const(shape):
        nd = len(shape)
        return pl.BlockSpec(shape, lambda bb, ii, _nd=nd: (0,) * _nd,
                            pipeline_mode=pl.Buffered(1))

    def upper_half(shape):
        nd = len(shape)
        return pl.BlockSpec(shape, lambda bb, ii, _nd=nd: (N_A // n_b,) + (0,) * (_nd - 1),
                            pipeline_mode=pl.Buffered(1))

    tile = lambda: pl.BlockSpec((None, tm, d), lambda bb, ii: (bb, ii, 0))
    return pl.pallas_call(
        _b_layers_kernel,
        grid=(b, n_steps),
        in_specs=[
            tile(),
            pl.BlockSpec((n_b, tm, PLE_DIM), lambda bb, ii: (N_A // n_b, bb * n_steps + ii, 0)),
            upper_half((n_b, 1, d)),
            const((n_b, d, 2 * d)),
            const((n_b, d, d)),
            upper_half((n_b, d, d)),
            upper_half((n_b, PLE_DIM, d)),
            const((1, d)),
            const((1, d)),
            const((d, 2 * d)),
        ],
        out_specs=tile(),
        out_shape=jax.ShapeDtypeStruct((b, s, d), _F32),
        scratch_shapes=[
            pltpu.VMEM((s // TK, d, TK), _BF16),
            pltpu.VMEM((s, d), _BF16),
            pltpu.VMEM((TILES, TQ, d), _BF16),
            pltpu.VMEM((TILES, TQ, d), _BF16),
            pltpu.VMEM((TILES, TQ, d), _BF16),
        ],
        compiler_params=pltpu.CompilerParams(
            dimension_semantics=("arbitrary", "arbitrary"),
            vmem_limit_bytes=VMEM_LIMIT),
        name="b_layers",
    )(h3, p2, norm_g, w_in, w_out, gate_w, ple_w, final_g, kv_g, w_kv)


def kernel(x, p, norm_g, a_w_in, a_ln_g, a_ln_b, a_w_s, a_b_s, a_w_out, kv_norm_g, w_kv, b_w_in, b_w_out, ple_w, ple_gate_w, final_g):
    b, s, d = x.shape
    t = b * s
    n_b = DEPTH - N_A
    bf = lambda w: w.astype(_BF16)

    h = x.reshape(t, d)
    p2 = p.reshape(DEPTH, t, PLE_DIM)
    norm_g3 = norm_g.reshape(DEPTH, 1, d)
    ple_wb = bf(ple_w)
    small = (norm_g3, a_ln_g.reshape(N_A, 1, -1), a_ln_b.reshape(N_A, 1, -1), a_w_s,
             jnp.swapaxes(a_b_s, 1, 2), ple_wb)

    n_steps = t // TM_A
    slab = lambda rows: rows // n_steps
    casts = (
        (a_w_in.reshape(N_A * d, -1), slab(d), n_steps),
        (a_w_out.reshape(N_A * A_WIDTH, d), slab(A_WIDTH), n_steps),
        (b_w_in.reshape(n_b * d, -1), slab(n_b * d), 0),
        (b_w_out.reshape(n_b * d, d), slab(n_b * d), 0),
        (w_kv, slab(d), 0),
        (ple_gate_w.reshape(DEPTH * d, d), slab(DEPTH * d), 0),
    )
    h, (w_in1, w_out1, b_w_in_b, b_w_out_b, w_kv_b, gate_b) = _a_layer(
        h, p2, 0, *small, bf(a_w_in[0])[None], bf(a_w_out[0])[None], bf(ple_gate_w[0])[None], 0,
        casts=casts)
    gate_b = gate_b.reshape(DEPTH, d, d)
    for i in range(1, N_A):
        h, _ = _a_layer(h, p2, i, *small, w_in1[None], w_out1[None], gate_b, i)

    return _b_layers(h.reshape(b, s, d), p2, norm_g3, b_w_in_b.reshape(n_b, d, -1),
                     b_w_out_b.reshape(n_b, d, d), gate_b, ple_wb,
                     final_g.reshape(1, d), kv_norm_g.reshape(1, d), w_kv_b)
```

```python
import functools

import jax
import jax.numpy as jnp
import numpy as np
from jax import lax
from jax.experimental import pallas as pl
from jax.experimental.pallas import tpu as pltpu

D_MODEL = 1024
DEPTH = 4
N_A = DEPTH // 2
CHUNK = 64
GM_BLOCK = 128
A_WIDTH = 2 * D_MODEL
A_GROUPS = 8
A_GROUP_CH = A_WIDTH // A_GROUPS
B_HEADS = 8
B_HEAD_DIM = D_MODEL // B_HEADS
PLE_DIM = 256
EPS = 1e-6

COLS = 256
TM_A = 512
TQ = 256
TK = 256
TILES = 2
HEAD_GROUPS = 4
VMEM_LIMIT = 60 * 1024 * 1024

_SQRT_HALF = float(np.sqrt(0.5))
_Q_SCALE = float(1.0 / np.sqrt(B_HEAD_DIM))
_EXP_ZERO_CUTOFF = -105.0
_HUGE_CARRY = 1e30
_NEG_LOG2E = float(-1.0 / np.log(2.0))
_F32 = jnp.float32
_BF16 = jnp.bfloat16


def _rms(x, g):
    ms = jnp.mean(x * x, axis=-1, keepdims=True)
    return x * lax.rsqrt(ms + EPS) * g


def _gelu(x):
    return 0.5 * x * (1.0 + lax.erf(x * _SQRT_HALF))


def _silu(x):
    return x * jax.nn.sigmoid(x)


def _dot(a, b):
    return jnp.dot(a, b, preferred_element_type=_F32)


_HEAD_SLICES = [slice(hh * B_HEAD_DIM, (hh + 1) * B_HEAD_DIM) for hh in range(B_HEADS)]
_COL_SLICES = [slice(c * COLS, (c + 1) * COLS) for c in range(D_MODEL // COLS)]


def _interleave(primary, filler):
    done = 0
    for k, step in enumerate(primary):
        step()
        upto = (k + 1) * len(filler) // len(primary)
        for f in filler[done:upto]:
            f()
        done = upto


def _a_layer_kernel(n_cast, h_ref, p_ref, ng_ref, w_in_ref, lng_ref, lnb_ref, ws_ref,
                    bst_ref, w_out_ref, gate_w_ref, ple_w_ref, *refs):
    cast_in, out_ref, cast_out = refs[:n_cast], refs[n_cast], refs[n_cast + 1:2 * n_cast + 1]
    hn_ref, v_ref, vn_ref, ug_ref, y_ref, h1b_ref = refs[2 * n_cast + 1:]
    tm = h_ref.shape[0]
    for src, dst in zip(cast_in, cast_out):
        dst[...] = src[...].astype(_BF16)

    n_grp = tm // GM_BLOCK
    chunks = [slice(c * COLS, (c + 1) * COLS) for c in range(A_WIDTH // COLS)]

    hn_ref[...] = _rms(h_ref[...], ng_ref[...]).astype(_BF16)

    stat = {"s1": jnp.zeros((tm, 1), _F32)}
    for c, cs in enumerate(chunks):
        vc = _gelu(_dot(hn_ref[...], w_in_ref[:, A_WIDTH + c * COLS:A_WIDTH + (c + 1) * COLS]))
        v_ref[:, cs] = vc
        stat["s1"] = stat["s1"] + jnp.sum(vc, axis=-1, keepdims=True)

    def mean():
        stat["mu"] = stat["s1"] * (1.0 / A_WIDTH)
        stat["s2"] = jnp.zeros((tm, 1), _F32)

    def center(cs):
        d = v_ref[:, cs] - stat["mu"]
        stat["s2"] = stat["s2"] + jnp.sum(d * d, axis=-1, keepdims=True)

    def scale():
        stat["rstd"] = lax.rsqrt(stat["s2"] * (1.0 / A_WIDTH) + EPS)

    def normalize(cs):
        vn = (v_ref[:, cs] - stat["mu"]) * stat["rstd"] * lng_ref[:, cs] + lnb_ref[:, cs]
        vn_ref[:, cs] = vn.astype(_BF16)

    def u_gate(g):
        cs = slice(g * A_GROUP_CH, (g + 1) * A_GROUP_CH)
        u = _gelu(_dot(hn_ref[...], w_in_ref[:, cs]))
        gate = _dot(hn_ref[...], w_in_ref[:, 2 * A_WIDTH + g * A_GROUP_CH:2 * A_WIDTH + (g + 1) * A_GROUP_CH])
        ug_ref[:, cs] = u * _silu(gate)

    layernorm = ([mean] + [lambda cs=cs: center(cs) for cs in chunks] + [scale]
                 + [lambda cs=cs: normalize(cs) for cs in chunks])
    _interleave([lambda g=g: u_gate(g) for g in range(A_GROUPS)], layernorm)

    t_chunk = lax.broadcasted_iota(jnp.int32, (GM_BLOCK, GM_BLOCK), 0) // CHUNK
    s_chunk = lax.broadcasted_iota(jnp.int32, (GM_BLOCK, GM_BLOCK), 1) // CHUNK
    mask = s_chunk <= t_chunk

    for g in range(A_GROUPS):
        cs = slice(g * A_GROUP_CH, (g + 1) * A_GROUP_CH)
        wm = jnp.where(mask, ws_ref[g], 0.0).astype(_BF16)
        b_col = bst_ref[:, g:g + 1]
        sv = jnp.concatenate(
            [_dot(wm, vn_ref[n * GM_BLOCK:(n + 1) * GM_BLOCK, cs]) + b_col
             for n in range(n_grp)], axis=0)
        y_ref[:, cs] = (ug_ref[:, cs] * sv).astype(_BF16)

    for cs in _COL_SLICES:
        h1 = h_ref[:, cs] + _dot(y_ref[...], w_out_ref[:, cs])
        out_ref[:, cs] = h1
        h1b_ref[:, cs] = h1.astype(_BF16)
    pb = p_ref[...].astype(_BF16)
    for cs in _COL_SLICES:
        pg = _dot(h1b_ref[...], gate_w_ref[:, cs])
        pe = _dot(pb, ple_w_ref[:, cs])
        out_ref[:, cs] = out_ref[:, cs] + jax.nn.sigmoid(pg) * pe


def _a_layer(h, p2, layer, norm_g, ln_g, ln_b, w_s, b_s_t, ple_w, w_in, w_out, gate_w, gate_layer, casts=()):
    t = h.shape[0]
    tm = TM_A
    n_steps = t // tm

    def param(index, *shape):
        return pl.BlockSpec((None,) + shape, lambda i: (index,) + (0,) * len(shape),
                            pipeline_mode=pl.Buffered(1))

    cast_in, cast_out, cast_shapes = [], [], []
    for src, slab, first in casts:
        cols = src.shape[1]
        cast_in.append(pl.BlockSpec((slab, cols), lambda i, _f=first: (_f + i, 0)))
        cast_out.append(pl.BlockSpec((slab, cols), lambda i: (i, 0)))
        cast_shapes.append(jax.ShapeDtypeStruct((slab * n_steps, cols), _BF16))

    out = pl.pallas_call(
        functools.partial(_a_layer_kernel, len(casts)),
        grid=(n_steps,),
        in_specs=[
            pl.BlockSpec((tm, D_MODEL), lambda i: (i, 0)),
            pl.BlockSpec((None, tm, PLE_DIM), lambda i: (layer, i, 0)),
            param(layer, 1, D_MODEL),
            param(0, D_MODEL, 3 * A_WIDTH),
            param(layer, 1, A_WIDTH),
            param(layer, 1, A_WIDTH),
            param(layer, A_GROUPS, GM_BLOCK, GM_BLOCK),
            param(layer, GM_BLOCK, A_GROUPS),
            param(0, A_WIDTH, D_MODEL),
            param(gate_layer, D_MODEL, D_MODEL),
            param(layer, PLE_DIM, D_MODEL),
        ] + cast_in,
        out_specs=[pl.BlockSpec((tm, D_MODEL), lambda i: (i, 0))] + cast_out,
        out_shape=[jax.ShapeDtypeStruct((t, D_MODEL), _F32)] + cast_shapes,
        scratch_shapes=[
            pltpu.VMEM((tm, D_MODEL), _BF16),
            pltpu.VMEM((tm, A_WIDTH), _F32),
            pltpu.VMEM((tm, A_WIDTH), _BF16),
            pltpu.VMEM((tm, A_WIDTH), _F32),
            pltpu.VMEM((tm, A_WIDTH), _BF16),
            pltpu.VMEM((tm, D_MODEL), _BF16),
        ],
        compiler_params=pltpu.CompilerParams(
            dimension_semantics=("arbitrary",), vmem_limit_bytes=VMEM_LIMIT),
        name="a_layer",
    )(h, p2, norm_g, w_in, ln_g, ln_b, w_s, b_s_t, w_out, gate_w, ple_w, *[c[0] for c in casts])
    return out[0], out[1:]


def _b_layers_kernel(h_ref, p_ref, ng_ref, w_in_ref, w_out_ref, gate_w_ref,
                     ple_w_ref, fg_ref, kvg_ref, w_kv_ref, out_ref,
                     kt_ref, v_ref, hn_ref, q_ref, sg_ref):
    step = pl.program_id(1)
    n_b = DEPTH - N_A
    heads = _HEAD_SLICES
    rows = [slice(t * TQ, (t + 1) * TQ) for t in range(TILES)]
    blocks = [step * TILES + t for t in range(TILES)]

    half = TK // 2
    row = lax.broadcasted_iota(jnp.int32, (TK, TK), 0)
    col = lax.broadcasted_iota(jnp.int32, (TK, TK), 1)
    upper = {TK: jnp.where(row > col, 1.0, 0.0).astype(_BF16)}
    upper[half] = upper[TK][:half, :half]
    causal_top = (col < row)[:half, :half]
    causal_bottom = (col < row)[half:, :]

    def sweep_steps(items, tots, results):
        zs, nls, excl = {}, {}, {}

        def logits(k):
            q, h, blk, kn, mask = items[k][:5]
            z = _dot(q(), kt_ref[blk, heads[h], :kn])
            zs[k] = z if mask is None else jnp.where(mask, z, -_HUGE_CARRY)

        def softplus(k):
            z = zs[k]
            nls[k] = jnp.maximum(z, 0.0) + jnp.log(1.0 + jnp.exp2(jnp.abs(z) * _NEG_LOG2E))

        def suffix(k):
            excl[k] = _dot(nls[k].astype(_BF16), upper[items[k][3]])
            tots[k] = excl[k][:, 0:1] + nls[k][:, 0:1]

        def weigh(k):
            _, h, blk, kn, _, carry = items[k]
            below = nls.pop(k) + excl.pop(k)
            if carry is not None:
                below = below + carry()
            a = jnp.exp(zs.pop(k) - below)
            k0 = pl.multiple_of(blk * TK, TK)
            results[k] = _dot(a.astype(_BF16), v_ref[pl.ds(k0, kn), heads[h]])

        order = range(len(items))
        return [lambda k=k, f=f: f(k) for f in (logits, softplus, suffix, weigh) for k in order]

    def live(carries):
        low = carries[0]
        for c in carries[1:]:
            low = jnp.minimum(low, c)
        return (jnp.min(low) < -_EXP_ZERO_CUTOFF).astype(jnp.int32)

    def window_steps(t, holder):
        blk = blocks[t]
        prev = jnp.maximum(blk - 1, 0)
        no_prev = jnp.where(blk > 0, 0.0, _HUGE_CARRY)
        d_items, d_tots, d_pvs, diag = [], {}, {}, {}
        for h, hs in enumerate(heads):
            d_items.append((lambda hs=hs: q_ref[t, :half, hs], h, blk, half, causal_top, None))
            d_items.append((lambda hs=hs: q_ref[t, half:, hs], h, blk, TK, causal_bottom, None))

        def diag_carry(h):
            if h not in diag:
                diag[h] = jnp.concatenate([d_tots[2 * h], d_tots[2 * h + 1]], axis=0) + no_prev
            return diag[h]

        p_items = [(lambda hs=hs: q_ref[t, :, hs], h, prev, TK, None,
                    lambda h=h: diag_carry(h)) for h, hs in enumerate(heads)]
        p_tots, p_pvs = {}, {}
        d_steps = sweep_steps(d_items, d_tots, d_pvs)
        p_steps = sweep_steps(p_items, p_tots, p_pvs)
        nd, npv = len(d_items), len(p_items)

        def carries():
            holder["carry"] = [diag_carry(h) + p_tots[h] for h in range(B_HEADS)]
            holder["go"] = live(holder["carry"])

        def finish():
            holder["acc"] = [
                jnp.concatenate([d_pvs[2 * h], d_pvs[2 * h + 1]], axis=0) + p_pvs[h]
                for h in range(B_HEADS)]

        ordered = []
        per = B_HEADS // HEAD_GROUPS
        for grp in range(HEAD_GROUPS):
            hs_ = range(grp * per, (grp + 1) * per)
            for stage in range(4):
                for h in hs_:
                    ordered += d_steps[stage * nd + 2 * h:stage * nd + 2 * h + 2]
                for h in hs_:
                    ordered.append(p_steps[stage * npv + h])
                if stage == 2 and grp == HEAD_GROUPS - 1:
                    ordered.append(carries)
        return ordered + [finish]

    def finish_sweep(t, holder):
        def full_block(j, carry, acc):
            tots, pvs = {}, {}
            items = [(lambda hs=hs: q_ref[t, :, hs], h, j, TK, None,
                      lambda h=h: carry[h]) for h, hs in enumerate(heads)]
            for f in sweep_steps(items, tots, pvs):
                f()
            return ([carry[h] + tots[h] for h in range(B_HEADS)],
                    [acc[h] + pvs[h] for h in range(B_HEADS)])

        def cond(loop):
            return jnp.logical_and(loop[0] >= 0, loop[1] > 0)

        def body(loop):
            carry, acc = full_block(loop[0], loop[2], loop[3])
            return loop[0] - 1, live(carry), carry, acc

        out = lax.while_loop(cond, body, (blocks[t] - 2, holder["go"], holder["carry"], holder["acc"]))
        holder["acc"] = out[3]

    def kv_steps():
        r0 = pl.multiple_of(blocks[0] * TQ, TILES * TQ)

        def norm():
            for t in range(TILES):
                hn_ref[t] = _rms(h_ref[rows[t], :], kvg_ref[...]).astype(_BF16)

        def chunk(c):
            kv = _dot(hn_ref[...].reshape(TILES * TQ, D_MODEL), w_kv_ref[:, c * COLS:(c + 1) * COLS])
            cs = _COL_SLICES[c % (D_MODEL // COLS)]
            if c < D_MODEL // COLS:
                for t in range(TILES):
                    kt_ref[blocks[t], cs, :] = kv[rows[t]].T.astype(_BF16)
            else:
                v_ref[pl.ds(r0, TILES * TQ), cs] = kv.astype(_BF16)

        return [norm] + [lambda c=c: chunk(c) for c in range(2 * D_MODEL // COLS)]

    def stream(j):
        return h_ref if j == 0 else out_ref

    def pre_steps(t, j):
        def norm():
            hn_ref[t] = _rms(stream(j)[rows[t], :], ng_ref[j]).astype(_BF16)

        def query(c):
            cs = _COL_SLICES[c]
            q_ref[t, :, cs] = (_dot(hn_ref[t], w_in_ref[j, :, cs]) * _Q_SCALE).astype(_BF16)

        def gate(c):
            cs = _COL_SLICES[c]
            g = _dot(hn_ref[t], w_in_ref[j, :, D_MODEL + c * COLS:D_MODEL + (c + 1) * COLS])
            sg_ref[t, :, cs] = _silu(g).astype(_BF16)

        n_c = len(_COL_SLICES)
        return ([norm] + [lambda c=c: query(c) for c in range(n_c)]
                + [lambda c=c: gate(c) for c in range(n_c)])

    def post_steps(t, j, holder):
        def gated():
            for hh, hs in enumerate(heads):
                sg_ref[t, :, hs] = holder["acc"][hh].astype(_BF16) * sg_ref[t, :, hs]

        def mix(c):
            cs = _COL_SLICES[c]
            h1 = stream(j)[rows[t], cs] + _dot(sg_ref[t], w_out_ref[j, :, cs])
            out_ref[rows[t], cs] = h1
            hn_ref[t, :, cs] = h1.astype(_BF16)

        def ple(c):
            cs = _COL_SLICES[c]
            pg = _dot(hn_ref[t], gate_w_ref[j, :, cs])
            pe = _dot(p_ref[j, rows[t], :].astype(_BF16), ple_w_ref[j, :, cs])
            out_ref[rows[t], cs] = out_ref[rows[t], cs] + jax.nn.sigmoid(pg) * pe

        n_c = len(_COL_SLICES)
        return ([gated] + [lambda c=c: mix(c) for c in range(n_c)]
                + [lambda c=c: ple(c) for c in range(n_c)])

    def program(sweep):
        x, y = 0, 1

        def settle(t, holder, more):
            if sweep:
                finish_sweep(t, holder)
                return more
            return jnp.maximum(more, jnp.where(blocks[t] >= 2, holder["go"], 0))

        def layer(j, more):
            att = [{} for _ in range(TILES)]
            for f in pre_steps(x, j):
                f()
            _interleave(window_steps(x, att[x]), pre_steps(y, j))
            more = settle(x, att[x], more)
            _interleave(window_steps(y, att[y]), post_steps(x, j, att[x]))
            more = settle(y, att[y], more)
            for f in post_steps(y, j, att[y]):
                f()
            return more

        for f in kv_steps():
            f()
        more = jnp.int32(0)
        for j in range(n_b):
            more = layer(j, more)
        for t in range(TILES):
            out_ref[rows[t], :] = _rms(out_ref[rows[t], :], fg_ref[...])
        return more

    redo = program(sweep=False)

    @pl.when(redo > 0)
    def _():
        program(sweep=True)


def _b_layers(h3, p2, norm_g, w_in, w_out, gate_w, ple_w, final_g, kv_g, w_kv):
    b, s, d = h3.shape
    tm = TILES * TQ
    n_steps = s // tm
    n_b = DEPTH - N_A

    def const(shape):
        nd = len(shape)
        return pl.BlockSpec(shape, lambda bb, ii, _nd=nd: (0,) * _nd,
                            pipeline_mode=pl.Buffered(1))

    def upper_half(shape):
        nd = len(shape)
        return pl.BlockSpec(shape, lambda bb, ii, _nd=nd: (N_A // n_b,) + (0,) * (_nd - 1),
                            pipeline_mode=pl.Buffered(1))

    tile = lambda: pl.BlockSpec((None, tm, d), lambda bb, ii: (bb, ii, 0))
    return pl.pallas_call(
        _b_layers_kernel,
        grid=(b, n_steps),
        in_specs=[
            tile(),
            pl.BlockSpec((n_b, tm, PLE_DIM), lambda bb, ii: (N_A // n_b, bb * n_steps + ii, 0)),
            upper_half((n_b, 1, d)),
            const((n_b, d, 2 * d)),
            const((n_b, d, d)),
            upper_half((n_b, d, d)),
            upper_half((n_b, PLE_DIM, d)),
            const((1, d)),
            const((1, d)),
            const((d, 2 * d)),
        ],
        out_specs=tile(),
        out_shape=jax.ShapeDtypeStruct((b, s, d), _F32),
        scratch_shapes=[
            pltpu.VMEM((s // TK, d, TK), _BF16),
            pltpu.VMEM((s, d), _BF16),
            pltpu.VMEM((TILES, TQ, d), _BF16),
            pltpu.VMEM((TILES, TQ, d), _BF16),
            pltpu.VMEM((TILES, TQ, d), _BF16),
        ],
        compiler_params=pltpu.CompilerParams(
            dimension_semantics=("arbitrary", "arbitrary"),
            vmem_limit_bytes=VMEM_LIMIT),
        name="b_layers",
    )(h3, p2, norm_g, w_in, w_out, gate_w, ple_w, final_g, kv_g, w_kv)


def kernel(x, p, norm_g, a_w_in, a_ln_g, a_ln_b, a_w_s, a_b_s, a_w_out, kv_norm_g, w_kv, b_w_in, b_w_out, ple_w, ple_gate_w, final_g):
    b, s, d = x.shape
    t = b * s
    n_b = DEPTH - N_A
    bf = lambda w: w.astype(_BF16)

    h = x.reshape(t, d)
    p2 = p.reshape(DEPTH, t, PLE_DIM)
    norm_g3 = norm_g.reshape(DEPTH, 1, d)
    ple_wb = bf(ple_w)
    small = (norm_g3, a_ln_g.reshape(N_A, 1, -1), a_ln_b.reshape(N_A, 1, -1), a_w_s,
             jnp.swapaxes(a_b_s, 1, 2), ple_wb)

    n_steps = t // TM_A
    slab = lambda rows: rows // n_steps
    casts = (
        (a_w_in.reshape(N_A * d, -1), slab(d), n_steps),
        (a_w_out.reshape(N_A * A_WIDTH, d), slab(A_WIDTH), n_steps),
        (b_w_in.reshape(n_b * d, -1), slab(n_b * d), 0),
        (b_w_out.reshape(n_b * d, d), slab(n_b * d), 0),
        (w_kv, slab(d), 0),
        (ple_gate_w.reshape(DEPTH * d, d), slab(DEPTH * d), 0),
    )
    h, (w_in1, w_out1, b_w_in_b, b_w_out_b, w_kv_b, gate_b) = _a_layer(
        h, p2, 0, *small, bf(a_w_in[0])[None], bf(a_w_out[0])[None], bf(ple_gate_w[0])[None], 0,
        casts=casts)
    gate_b = gate_b.reshape(DEPTH, d, d)
    for i in range(1, N_A):
        h, _ = _a_layer(h, p2, i, *small, w_in1[None], w_out1[None], gate_b, i)

    return _b_layers(h.reshape(b, s, d), p2, norm_g3, b_w_in_b.reshape(n_b, d, -1),
                     b_w_out_b.reshape(n_b, d, d), gate_b, ple_wb,
                     final_g.reshape(1, d), kv_norm_g.reshape(1, d), w_kv_b)
```

```python
import functools

import jax
import jax.numpy as jnp
import numpy as np
from jax import lax
from jax.experimental import pallas as pl
from jax.experimental.pallas import tpu as pltpu

D_MODEL = 1024
DEPTH = 4
N_A = DEPTH // 2
CHUNK = 64
GM_BLOCK = 128
A_WIDTH = 2 * D_MODEL
A_GROUPS = 8
A_GROUP_CH = A_WIDTH // A_GROUPS
B_HEADS = 8
B_HEAD_DIM = D_MODEL // B_HEADS
PLE_DIM = 256
EPS = 1e-6

COLS = 256
TM_A = 512
TQ = 256
TK = 256
TILES = 2
HEAD_GROUPS = 4
VMEM_LIMIT = 60 * 1024 * 1024

_SQRT_HALF = float(np.sqrt(0.5))
_Q_SCALE = float(1.0 / np.sqrt(B_HEAD_DIM))
_EXP_ZERO_CUTOFF = -105.0
_HUGE_CARRY = 1e30
_NEG_LOG2E = float(-1.0 / np.log(2.0))
_F32 = jnp.float32
_BF16 = jnp.bfloat16


def _rms(x, g):
    ms = jnp.mean(x * x, axis=-1, keepdims=True)
    return x * lax.rsqrt(ms + EPS) * g


def _gelu(x):
    return 0.5 * x * (1.0 + lax.erf(x * _SQRT_HALF))


def _silu(x):
    return x * jax.nn.sigmoid(x)


def _dot(a, b):
    return jnp.dot(a, b, preferred_element_type=_F32)


_HEAD_SLICES = [slice(hh * B_HEAD_DIM, (hh + 1) * B_HEAD_DIM) for hh in range(B_HEADS)]
_COL_SLICES = [slice(c * COLS, (c + 1) * COLS) for c in range(D_MODEL // COLS)]


def _interleave(primary, filler):
    done = 0
    for k, step in enumerate(primary):
        step()
        upto = (k + 1) * len(filler) // len(primary)
        for f in filler[done:upto]:
            f()
        done = upto


def _a_layer_kernel(n_cast, h_ref, p_ref, ng_ref, w_in_ref, lng_ref, lnb_ref, ws_ref,
                    bst_ref, w_out_ref, gate_w_ref, ple_w_ref, *refs):
    cast_in, out_ref, cast_out = refs[:n_cast], refs[n_cast], refs[n_cast + 1:2 * n_cast + 1]
    hn_ref, v_ref, vn_ref, ug_ref, y_ref, h1b_ref, pe_ref = refs[2 * n_cast + 1:]
    tm = h_ref.shape[0]
    for src, dst in zip(cast_in, cast_out):
        dst[...] = src[...].astype(_BF16)

    n_grp = tm // GM_BLOCK
    chunks = [slice(c * COLS, (c + 1) * COLS) for c in range(A_WIDTH // COLS)]

    pb = p_ref[...].astype(_BF16)
    for cs in _COL_SLICES:
        pe_ref[:, cs] = _dot(pb, ple_w_ref[:, cs])

    hn_ref[...] = _rms(h_ref[...], ng_ref[...]).astype(_BF16)

    stat = {"s1": jnp.zeros((tm, 1), _F32)}
    for c, cs in enumerate(chunks):
        vc = _gelu(_dot(hn_ref[...], w_in_ref[:, A_WIDTH + c * COLS:A_WIDTH + (c + 1) * COLS]))
        v_ref[:, cs] = vc
        stat["s1"] = stat["s1"] + jnp.sum(vc, axis=-1, keepdims=True)

    def mean():
        stat["mu"] = stat["s1"] * (1.0 / A_WIDTH)
        stat["s2"] = jnp.zeros((tm, 1), _F32)

    def center(cs):
        d = v_ref[:, cs] - stat["mu"]
        stat["s2"] = stat["s2"] + jnp.sum(d * d, axis=-1, keepdims=True)

    def scale():
        stat["rstd"] = lax.rsqrt(stat["s2"] * (1.0 / A_WIDTH) + EPS)

    def normalize(cs):
        vn = (v_ref[:, cs] - stat["mu"]) * stat["rstd"] * lng_ref[:, cs] + lnb_ref[:, cs]
        vn_ref[:, cs] = vn.astype(_BF16)

    def u_gate(g):
        cs = slice(g * A_GROUP_CH, (g + 1) * A_GROUP_CH)
        u = _gelu(_dot(hn_ref[...], w_in_ref[:, cs]))
        gate = _dot(hn_ref[...], w_in_ref[:, 2 * A_WIDTH + g * A_GROUP_CH:2 * A_WIDTH + (g + 1) * A_GROUP_CH])
        ug_ref[:, cs] = u * _silu(gate)

    layernorm = ([mean] + [lambda cs=cs: center(cs) for cs in chunks] + [scale]
                 + [lambda cs=cs: normalize(cs) for cs in chunks])
    _interleave([lambda g=g: u_gate(g) for g in range(A_GROUPS)], layernorm)

    t_chunk = lax.broadcasted_iota(jnp.int32, (GM_BLOCK, GM_BLOCK), 0) // CHUNK
    s_chunk = lax.broadcasted_iota(jnp.int32, (GM_BLOCK, GM_BLOCK), 1) // CHUNK
    mask = s_chunk <= t_chunk

    for g in range(A_GROUPS):
        cs = slice(g * A_GROUP_CH, (g + 1) * A_GROUP_CH)
        wm = jnp.where(mask, ws_ref[g], 0.0).astype(_BF16)
        b_col = bst_ref[:, g:g + 1]
        sv = jnp.concatenate(
            [_dot(wm, vn_ref[n * GM_BLOCK:(n + 1) * GM_BLOCK, cs]) + b_col
             for n in range(n_grp)], axis=0)
        y_ref[:, cs] = (ug_ref[:, cs] * sv).astype(_BF16)

    for cs in _COL_SLICES:
        h1 = h_ref[:, cs] + _dot(y_ref[...], w_out_ref[:, cs])
        out_ref[:, cs] = h1
        h1b_ref[:, cs] = h1.astype(_BF16)
    for cs in _COL_SLICES:
        pg = _dot(h1b_ref[...], gate_w_ref[:, cs])
        out_ref[:, cs] = out_ref[:, cs] + jax.nn.sigmoid(pg) * pe_ref[:, cs]


def _a_layer(h, p2, layer, norm_g, ln_g, ln_b, w_s, b_s_t, ple_w, w_in, w_out, gate_w, gate_layer, casts=()):
    t = h.shape[0]
    tm = TM_A
    n_steps = t // tm

    def param(index, *shape):
        return pl.BlockSpec((None,) + shape, lambda i: (index,) + (0,) * len(shape),
                            pipeline_mode=pl.Buffered(1))

    cast_in, cast_out, cast_shapes = [], [], []
    for src, slab, first in casts:
        cols = src.shape[1]
        cast_in.append(pl.BlockSpec((slab, cols), lambda i, _f=first: (_f + i, 0)))
        cast_out.append(pl.BlockSpec((slab, cols), lambda i: (i, 0)))
        cast_shapes.append(jax.ShapeDtypeStruct((slab * n_steps, cols), _BF16))

    out = pl.pallas_call(
        functools.partial(_a_layer_kernel, len(casts)),
        grid=(n_steps,),
        in_specs=[
            pl.BlockSpec((tm, D_MODEL), lambda i: (i, 0)),
            pl.BlockSpec((None, tm, PLE_DIM), lambda i: (layer, i, 0)),
            param(layer, 1, D_MODEL),
            param(0, D_MODEL, 3 * A_WIDTH),
            param(layer, 1, A_WIDTH),
            param(layer, 1, A_WIDTH),
            param(layer, A_GROUPS, GM_BLOCK, GM_BLOCK),
            param(layer, GM_BLOCK, A_GROUPS),
            param(0, A_WIDTH, D_MODEL),
            param(gate_layer, D_MODEL, D_MODEL),
            param(layer, PLE_DIM, D_MODEL),
        ] + cast_in,
        out_specs=[pl.BlockSpec((tm, D_MODEL), lambda i: (i, 0))] + cast_out,
        out_shape=[jax.ShapeDtypeStruct((t, D_MODEL), _F32)] + cast_shapes,
        scratch_shapes=[
            pltpu.VMEM((tm, D_MODEL), _BF16),
            pltpu.VMEM((tm, A_WIDTH), _F32),
            pltpu.VMEM((tm, A_WIDTH), _BF16),
            pltpu.VMEM((tm, A_WIDTH), _F32),
            pltpu.VMEM((tm, A_WIDTH), _BF16),
            pltpu.VMEM((tm, D_MODEL), _BF16),
            pltpu.VMEM((tm, D_MODEL), _F32),
        ],
        compiler_params=pltpu.CompilerParams(
            dimension_semantics=("arbitrary",), vmem_limit_bytes=VMEM_LIMIT),
        name="a_layer",
    )(h, p2, norm_g, w_in, ln_g, ln_b, w_s, b_s_t, w_out, gate_w, ple_w, *[c[0] for c in casts])
    return out[0], out[1:]


def _b_layers_kernel(h_ref, p_ref, ng_ref, w_in_ref, w_out_ref, gate_w_ref,
                     ple_w_ref, fg_ref, kvg_ref, w_kv_ref, out_ref,
                     kt_ref, v_ref, hn_ref, q_ref, sg_ref):
    step = pl.program_id(1)
    n_b = DEPTH - N_A
    heads = _HEAD_SLICES
    rows = [slice(t * TQ, (t + 1) * TQ) for t in range(TILES)]
    blocks = [step * TILES + t for t in range(TILES)]

    half = TK // 2
    row = lax.broadcasted_iota(jnp.int32, (TK, TK), 0)
    col = lax.broadcasted_iota(jnp.int32, (TK, TK), 1)
    upper = {TK: jnp.where(row > col, 1.0, 0.0).astype(_BF16)}
    upper[half] = upper[TK][:half, :half]
    causal_top = (col < row)[:half, :half]
    causal_bottom = (col < row)[half:, :]

    def sweep_steps(items, tots, results):
        zs, nls, excl = {}, {}, {}

        def logits(k):
            q, h, blk, kn, mask = items[k][:5]
            z = _dot(q(), kt_ref[blk, heads[h], :kn])
            zs[k] = z if mask is None else jnp.where(mask, z, -_HUGE_CARRY)

        def softplus(k):
            z = zs[k]
            nls[k] = jnp.maximum(z, 0.0) + jnp.log(1.0 + jnp.exp2(jnp.abs(z) * _NEG_LOG2E))

        def suffix(k):
            excl[k] = _dot(nls[k].astype(_BF16), upper[items[k][3]])
            tots[k] = excl[k][:, 0:1] + nls[k][:, 0:1]

        def weigh(k):
            _, h, blk, kn, _, carry = items[k]
            below = nls.pop(k) + excl.pop(k)
            if carry is not None:
                below = below + carry()
            a = jnp.exp(zs.pop(k) - below)
            k0 = pl.multiple_of(blk * TK, TK)
            results[k] = _dot(a.astype(_BF16), v_ref[pl.ds(k0, kn), heads[h]])

        order = range(len(items))
        return [lambda k=k, f=f: f(k) for f in (logits, softplus, suffix, weigh) for k in order]

    def live(carries):
        low = carries[0]
        for c in carries[1:]:
            low = jnp.minimum(low, c)
        return (jnp.min(low) < -_EXP_ZERO_CUTOFF).astype(jnp.int32)

    def window_steps(t, holder):
        blk = blocks[t]
        prev = jnp.maximum(blk - 1, 0)
        no_prev = jnp.where(blk > 0, 0.0, _HUGE_CARRY)
        d_items, d_tots, d_pvs, diag = [], {}, {}, {}
        for h, hs in enumerate(heads):
            d_items.append((lambda hs=hs: q_ref[t, :half, hs], h, blk, half, causal_top, None))
            d_items.append((lambda hs=hs: q_ref[t, half:, hs], h, blk, TK, causal_bottom, None))

        def diag_carry(h):
            if h not in diag:
                diag[h] = jnp.concatenate([d_tots[2 * h], d_tots[2 * h + 1]], axis=0) + no_prev
            return diag[h]

        p_items = [(lambda hs=hs: q_ref[t, :, hs], h, prev, TK, None,
                    lambda h=h: diag_carry(h)) for h, hs in enumerate(heads)]
        p_tots, p_pvs = {}, {}
        d_steps = sweep_steps(d_items, d_tots, d_pvs)
        p_steps = sweep_steps(p_items, p_tots, p_pvs)
        nd, npv = len(d_items), len(p_items)

        def carries():
            holder["carry"] = [diag_carry(h) + p_tots[h] for h in range(B_HEADS)]
            holder["go"] = live(holder["carry"])

        def finish():
            holder["acc"] = [
                jnp.concatenate([d_pvs[2 * h], d_pvs[2 * h + 1]], axis=0) + p_pvs[h]
                for h in range(B_HEADS)]

        ordered = []
        per = B_HEADS // HEAD_GROUPS
        for grp in range(HEAD_GROUPS):
            hs_ = range(grp * per, (grp + 1) * per)
            for stage in range(4):
                for h in hs_:
                    ordered += d_steps[stage * nd + 2 * h:stage * nd + 2 * h + 2]
                for h in hs_:
                    ordered.append(p_steps[stage * npv + h])
                if stage == 2 and grp == HEAD_GROUPS - 1:
                    ordered.append(carries)
        return ordered + [finish]

    def finish_sweep(t, holder):
        def full_block(j, carry, acc):
            tots, pvs = {}, {}
            items = [(lambda hs=hs: q_ref[t, :, hs], h, j, TK, None,
                      lambda h=h: carry[h]) for h, hs in enumerate(heads)]
            for f in sweep_steps(items, tots, pvs):
                f()
            return ([carry[h] + tots[h] for h in range(B_HEADS)],
                    [acc[h] + pvs[h] for h in range(B_HEADS)])

        def cond(loop):
            return jnp.logical_and(loop[0] >= 0, loop[1] > 0)

        def body(loop):
            carry, acc = full_block(loop[0], loop[2], loop[3])
            return loop[0] - 1, live(carry), carry, acc

        out = lax.while_loop(cond, body, (blocks[t] - 2, holder["go"], holder["carry"], holder["acc"]))
        holder["acc"] = out[3]

    def kv_steps():
        r0 = pl.multiple_of(blocks[0] * TQ, TILES * TQ)

        def norm():
            for t in range(TILES):
                hn_ref[t] = _rms(h_ref[rows[t], :], kvg_ref[...]).astype(_BF16)

        def chunk(c):
            kv = _dot(hn_ref[...].reshape(TILES * TQ, D_MODEL), w_kv_ref[:, c * COLS:(c + 1) * COLS])
            cs = _COL_SLICES[c % (D_MODEL // COLS)]
            if c < D_MODEL // COLS:
                for t in range(TILES):
                    kt_ref[blocks[t], cs, :] = kv[rows[t]].T.astype(_BF16)
            else:
                v_ref[pl.ds(r0, TILES * TQ), cs] = kv.astype(_BF16)

        return [norm] + [lambda c=c: chunk(c) for c in range(2 * D_MODEL // COLS)]

    def stream(j):
        return h_ref if j == 0 else out_ref

    def pre_steps(t, j):
        def norm():
            hn_ref[t] = _rms(stream(j)[rows[t], :], ng_ref[j]).astype(_BF16)

        def query(c):
            cs = _COL_SLICES[c]
            q_ref[t, :, cs] = (_dot(hn_ref[t], w_in_ref[j, :, cs]) * _Q_SCALE).astype(_BF16)

        def gate(c):
            cs = _COL_SLICES[c]
            g = _dot(hn_ref[t], w_in_ref[j, :, D_MODEL + c * COLS:D_MODEL + (c + 1) * COLS])
            sg_ref[t, :, cs] = _silu(g).astype(_BF16)

        n_c = len(_COL_SLICES)
        return ([norm] + [lambda c=c: query(c) for c in range(n_c)]
                + [lambda c=c: gate(c) for c in range(n_c)])

    def post_steps(t, j, holder):
        def gated():
            for hh, hs in enumerate(heads):
                sg_ref[t, :, hs] = holder["acc"][hh].astype(_BF16) * sg_ref[t, :, hs]

        def mix(c):
            cs = _COL_SLICES[c]
            h1 = stream(j)[rows[t], cs] + _dot(sg_ref[t], w_out_ref[j, :, cs])
            out_ref[rows[t], cs] = h1
            hn_ref[t, :, cs] = h1.astype(_BF16)

        def ple(c):
            cs = _COL_SLICES[c]
            pg = _dot(hn_ref[t], gate_w_ref[j, :, cs])
            pe = _dot(p_ref[j, rows[t], :].astype(_BF16), ple_w_ref[j, :, cs])
            out_ref[rows[t], cs] = out_ref[rows[t], cs] + jax.nn.sigmoid(pg) * pe

        n_c = len(_COL_SLICES)
        return ([gated] + [lambda c=c: mix(c) for c in range(n_c)]
                + [lambda c=c: ple(c) for c in range(n_c)])

    def program(sweep):
        x, y = 0, 1

        def settle(t, holder, more):
            if sweep:
                finish_sweep(t, holder)
                return more
            return jnp.maximum(more, jnp.where(blocks[t] >= 2, holder["go"], 0))

        def layer(j, more):
            att = [{} for _ in range(TILES)]
            for f in pre_steps(x, j):
                f()
            _interleave(window_steps(x, att[x]), pre_steps(y, j))
            more = settle(x, att[x], more)
            _interleave(window_steps(y, att[y]), post_steps(x, j, att[x]))
            more = settle(y, att[y], more)
            for f in post_steps(y, j, att[y]):
                f()
            return more

        for f in kv_steps():
            f()
        more = jnp.int32(0)
        for j in range(n_b):
            more = layer(j, more)
        for t in range(TILES):
            out_ref[rows[t], :] = _rms(out_ref[rows[t], :], fg_ref[...])
        return more

    redo = program(sweep=False)

    @pl.when(redo > 0)
    def _():
        program(sweep=True)


def _b_layers(h3, p2, norm_g, w_in, w_out, gate_w, ple_w, final_g, kv_g, w_kv):
    b, s, d = h3.shape
    tm = TILES * TQ
    n_steps = s // tm
    n_b = DEPTH - N_A

    def const(shape):
        nd = len(shape)
        return pl.BlockSpec(shape, lambda bb, ii, _nd=nd: (0,) * _nd,
                            pipeline_mode=pl.Buffered(1))

    def upper_half(shape):
        nd = len(shape)
        return pl.BlockSpec(shape, lambda bb, ii, _nd=nd: (N_A // n_b,) + (0,) * (_nd - 1),
                            pipeline_mode=pl.Buffered(1))

    tile = lambda: pl.BlockSpec((None, tm, d), lambda bb, ii: (bb, ii, 0))
    return pl.pallas_call(
        _b_layers_kernel,
        grid=(b, n_steps),
        in_specs=[
            tile(),
            pl.BlockSpec((n_b, tm, PLE_DIM), lambda bb, ii: (N_A // n_b, bb * n_steps + ii, 0)),
            upper_half((n_b, 1, d)),
            const((n_b, d, 2 * d)),
            const((n_b, d, d)),
            upper_half((n_b, d, d)),
            upper_half((n_b, PLE_DIM, d)),
            const((1, d)),
            const((1, d)),
            const((d, 2 * d)),
        ],
        out_specs=tile(),
        out_shape=jax.ShapeDtypeStruct((b, s, d), _F32),
        scratch_shapes=[
            pltpu.VMEM((s // TK, d, TK), _BF16),
            pltpu.VMEM((s, d), _BF16),
            pltpu.VMEM((TILES, TQ, d), _BF16),
            pltpu.VMEM((TILES, TQ, d), _BF16),
            pltpu.VMEM((TILES, TQ, d), _BF16),
        ],
        compiler_params=pltpu.CompilerParams(
            dimension_semantics=("arbitrary", "arbitrary"),
            vmem_limit_bytes=VMEM_LIMIT),
        name="b_layers",
    )(h3, p2, norm_g, w_in, w_out, gate_w, ple_w, final_g, kv_g, w_kv)


def kernel(x, p, norm_g, a_w_in, a_ln_g, a_ln_b, a_w_s, a_b_s, a_w_out, kv_norm_g, w_kv, b_w_in, b_w_out, ple_w, ple_gate_w, final_g):
    b, s, d = x.shape
    t = b * s
    n_b = DEPTH - N_A
    bf = lambda w: w.astype(_BF16)

    h = x.reshape(t, d)
    p2 = p.reshape(DEPTH, t, PLE_DIM)
    norm_g3 = norm_g.reshape(DEPTH, 1, d)
    ple_wb = bf(ple_w)
    small = (norm_g3, a_ln_g.reshape(N_A, 1, -1), a_ln_b.reshape(N_A, 1, -1), a_w_s,
             jnp.swapaxes(a_b_s, 1, 2), ple_wb)

    n_steps = t // TM_A
    slab = lambda rows: rows // n_steps
    casts = (
        (a_w_in.reshape(N_A * d, -1), slab(d), n_steps),
        (a_w_out.reshape(N_A * A_WIDTH, d), slab(A_WIDTH), n_steps),
        (b_w_in.reshape(n_b * d, -1), slab(n_b * d), 0),
        (b_w_out.reshape(n_b * d, d), slab(n_b * d), 0),
        (w_kv, slab(d), 0),
        (ple_gate_w.reshape(DEPTH * d, d), slab(DEPTH * d), 0),
    )
    h, (w_in1, w_out1, b_w_in_b, b_w_out_b, w_kv_b, gate_b) = _a_layer(
        h, p2, 0, *small, bf(a_w_in[0])[None], bf(a_w_out[0])[None], bf(ple_gate_w[0])[None], 0,
        casts=casts)
    gate_b = gate_b.reshape(DEPTH, d, d)
    for i in range(1, N_A):
        h, _ = _a_layer(h, p2, i, *small, w_in1[None], w_out1[None], gate_b, i)

    return _b_layers(h.reshape(b, s, d), p2, norm_g3, b_w_in_b.reshape(n_b, d, -1),
                     b_w_out_b.reshape(n_b, d, d), gate_b, ple_wb,
                     final_g.reshape(1, d), kv_norm_g.reshape(1, d), w_kv_b)
```

```python
import functools

import jax
import jax.numpy as jnp
import numpy as np
from jax import lax
from jax.experimental import pallas as pl
from jax.experimental.pallas import tpu as pltpu

D_MODEL = 1024
DEPTH = 4
N_A = DEPTH // 2
CHUNK = 64
GM_BLOCK = 128
A_WIDTH = 2 * D_MODEL
A_GROUPS = 8
A_GROUP_CH = A_WIDTH // A_GROUPS
B_HEADS = 8
B_HEAD_DIM = D_MODEL // B_HEADS
PLE_DIM = 256
EPS = 1e-6

COLS = 256
TM_A = 512
TQ = 256
TK = 256
TILES = 2
HEAD_GROUPS = 2
VMEM_LIMIT = 60 * 1024 * 1024

_SQRT_HALF = float(np.sqrt(0.5))
_Q_SCALE = float(1.0 / np.sqrt(B_HEAD_DIM))
_EXP_ZERO_CUTOFF = -105.0
_HUGE_CARRY = 1e30
_NEG_LOG2E = float(-1.0 / np.log(2.0))
_F32 = jnp.float32
_BF16 = jnp.bfloat16


def _rms(x, g):
    ms = jnp.mean(x * x, axis=-1, keepdims=True)
    return x * lax.rsqrt(ms + EPS) * g


def _gelu(x):
    return 0.5 * x * (1.0 + lax.erf(x * _SQRT_HALF))


def _silu(x):
    return x * jax.nn.sigmoid(x)


def _dot(a, b):
    return jnp.dot(a, b, preferred_element_type=_F32)


_HEAD_SLICES = [slice(hh * B_HEAD_DIM, (hh + 1) * B_HEAD_DIM) for hh in range(B_HEADS)]
_COL_SLICES = [slice(c * COLS, (c + 1) * COLS) for c in range(D_MODEL // COLS)]


def _interleave(primary, filler):
    done = 0
    for k, step in enumerate(primary):
        step()
        upto = (k + 1) * len(filler) // len(primary)
        for f in filler[done:upto]:
            f()
        done = upto


def _a_layer_kernel(n_cast, h_ref, p_ref, ng_ref, w_in_ref, lng_ref, lnb_ref, ws_ref,
                    bst_ref, w_out_ref, gate_w_ref, ple_w_ref, *refs):
    cast_in, out_ref, cast_out = refs[:n_cast], refs[n_cast], refs[n_cast + 1:2 * n_cast + 1]
    hn_ref, v_ref, vn_ref, ug_ref, y_ref, h1b_ref = refs[2 * n_cast + 1:]
    tm = h_ref.shape[0]
    for src, dst in zip(cast_in, cast_out):
        dst[...] = src[...].astype(_BF16)

    n_grp = tm // GM_BLOCK
    chunks = [slice(c * COLS, (c + 1) * COLS) for c in range(A_WIDTH // COLS)]

    hn_ref[...] = _rms(h_ref[...], ng_ref[...]).astype(_BF16)

    stat = {"s1": jnp.zeros((tm, 1), _F32)}
    for c, cs in enumerate(chunks):
        vc = _gelu(_dot(hn_ref[...], w_in_ref[:, A_WIDTH + c * COLS:A_WIDTH + (c + 1) * COLS]))
        v_ref[:, cs] = vc
        stat["s1"] = stat["s1"] + jnp.sum(vc, axis=-1, keepdims=True)

    def mean():
        stat["mu"] = stat["s1"] * (1.0 / A_WIDTH)
        stat["s2"] = jnp.zeros((tm, 1), _F32)

    def center(cs):
        d = v_ref[:, cs] - stat["mu"]
        stat["s2"] = stat["s2"] + jnp.sum(d * d, axis=-1, keepdims=True)

    def scale():
        stat["rstd"] = lax.rsqrt(stat["s2"] * (1.0 / A_WIDTH) + EPS)

    def normalize(cs):
        vn = (v_ref[:, cs] - stat["mu"]) * stat["rstd"] * lng_ref[:, cs] + lnb_ref[:, cs]
        vn_ref[:, cs] = vn.astype(_BF16)

    def u_gate(g):
        cs = slice(g * A_GROUP_CH, (g + 1) * A_GROUP_CH)
        u = _gelu(_dot(hn_ref[...], w_in_ref[:, cs]))
        gate = _dot(hn_ref[...], w_in_ref[:, 2 * A_WIDTH + g * A_GROUP_CH:2 * A_WIDTH + (g + 1) * A_GROUP_CH])
        ug_ref[:, cs] = u * _silu(gate)

    layernorm = ([mean] + [lambda cs=cs: center(cs) for cs in chunks] + [scale]
                 + [lambda cs=cs: normalize(cs) for cs in chunks])
    _interleave([lambda g=g: u_gate(g) for g in range(A_GROUPS)], layernorm)

    t_chunk = lax.broadcasted_iota(jnp.int32, (GM_BLOCK, GM_BLOCK), 0) // CHUNK
    s_chunk = lax.broadcasted_iota(jnp.int32, (GM_BLOCK, GM_BLOCK), 1) // CHUNK
    mask = s_chunk <= t_chunk

    for g in range(A_GROUPS):
        cs = slice(g * A_GROUP_CH, (g + 1) * A_GROUP_CH)
        wm = jnp.where(mask, ws_ref[g], 0.0).astype(_BF16)
        b_col = bst_ref[:, g:g + 1]
        sv = jnp.concatenate(
            [_dot(wm, vn_ref[n * GM_BLOCK:(n + 1) * GM_BLOCK, cs]) + b_col
             for n in range(n_grp)], axis=0)
        y_ref[:, cs] = (ug_ref[:, cs] * sv).astype(_BF16)

    for cs in _COL_SLICES:
        h1 = h_ref[:, cs] + _dot(y_ref[...], w_out_ref[:, cs])
        out_ref[:, cs] = h1
        h1b_ref[:, cs] = h1.astype(_BF16)
    pb = p_ref[...].astype(_BF16)
    for cs in _COL_SLICES:
        pg = _dot(h1b_ref[...], gate_w_ref[:, cs])
        pe = _dot(pb, ple_w_ref[:, cs])
        out_ref[:, cs] = out_ref[:, cs] + jax.nn.sigmoid(pg) * pe


def _a_layer(h, p2, layer, norm_g, ln_g, ln_b, w_s, b_s_t, ple_w, w_in, w_out, gate_w, gate_layer, casts=()):
    t = h.shape[0]
    tm = TM_A
    n_steps = t // tm

    def param(index, *shape):
        return pl.BlockSpec((None,) + shape, lambda i: (index,) + (0,) * len(shape),
                            pipeline_mode=pl.Buffered(1))

    cast_in, cast_out, cast_shapes = [], [], []
    for src, slab, first in casts:
        cols = src.shape[1]
        cast_in.append(pl.BlockSpec((slab, cols), lambda i, _f=first: (_f + i, 0)))
        cast_out.append(pl.BlockSpec((slab, cols), lambda i: (i, 0)))
        cast_shapes.append(jax.ShapeDtypeStruct((slab * n_steps, cols), _BF16))

    out = pl.pallas_call(
        functools.partial(_a_layer_kernel, len(casts)),
        grid=(n_steps,),
        in_specs=[
            pl.BlockSpec((tm, D_MODEL), lambda i: (i, 0)),
            pl.BlockSpec((None, tm, PLE_DIM), lambda i: (layer, i, 0)),
            param(layer, 1, D_MODEL),
            param(0, D_MODEL, 3 * A_WIDTH),
            param(layer, 1, A_WIDTH),
            param(layer, 1, A_WIDTH),
            param(layer, A_GROUPS, GM_BLOCK, GM_BLOCK),
            param(layer, GM_BLOCK, A_GROUPS),
            param(0, A_WIDTH, D_MODEL),
            param(gate_layer, D_MODEL, D_MODEL),
            param(layer, PLE_DIM, D_MODEL),
        ] + cast_in,
        out_specs=[pl.BlockSpec((tm, D_MODEL), lambda i: (i, 0))] + cast_out,
        out_shape=[jax.ShapeDtypeStruct((t, D_MODEL), _F32)] + cast_shapes,
        scratch_shapes=[
            pltpu.VMEM((tm, D_MODEL), _BF16),
            pltpu.VMEM((tm, A_WIDTH), _F32),
            pltpu.VMEM((tm, A_WIDTH), _BF16),
            pltpu.VMEM((tm, A_WIDTH), _F32),
            pltpu.VMEM((tm, A_WIDTH), _BF16),
            pltpu.VMEM((tm, D_MODEL), _BF16),
        ],
        compiler_params=pltpu.CompilerParams(
            dimension_semantics=("arbitrary",), vmem_limit_bytes=VMEM_LIMIT),
        name="a_layer",
    )(h, p2, norm_g, w_in, ln_g, ln_b, w_s, b_s_t, w_out, gate_w, ple_w, *[c[0] for c in casts])
    return out[0], out[1:]


def _b_layers_kernel(h_ref, p_ref, ng_ref, w_in_ref, w_out_ref, gate_w_ref,
                     ple_w_ref, fg_ref, kvg_ref, w_kv_ref, out_ref,
                     kt_ref, v_ref, hn_ref, q_ref, sg_ref):
    step = pl.program_id(1)
    n_b = DEPTH - N_A
    heads = _HEAD_SLICES
    rows = [slice(t * TQ, (t + 1) * TQ) for t in range(TILES)]
    blocks = [step * TILES + t for t in range(TILES)]

    half = TK // 2
    row = lax.broadcasted_iota(jnp.int32, (TK, TK), 0)
    col = lax.broadcasted_iota(jnp.int32, (TK, TK), 1)
    upper = {TK: jnp.where(row > col, 1.0, 0.0).astype(_BF16)}
    upper[half] = upper[TK][:half, :half]
    causal_top = (col < row)[:half, :half]
    causal_bottom = (col < row)[half:, :]

    def sweep_steps(items, tots, results):
        zs, nls, excl = {}, {}, {}

        def logits(k):
            q, h, blk, kn, mask = items[k][:5]
            z = _dot(q(), kt_ref[blk, heads[h], :kn])
            zs[k] = z if mask is None else jnp.where(mask, z, -_HUGE_CARRY)

        def softplus(k):
            z = zs[k]
            nls[k] = jnp.maximum(z, 0.0) + jnp.log(1.0 + jnp.exp2(jnp.abs(z) * _NEG_LOG2E))

        def suffix(k):
            excl[k] = _dot(nls[k].astype(_BF16), upper[items[k][3]])
            tots[k] = excl[k][:, 0:1] + nls[k][:, 0:1]

        def weigh(k):
            _, h, blk, kn, _, carry = items[k]
            below = nls.pop(k) + excl.pop(k)
            if carry is not None:
                below = below + carry()
            a = jnp.exp(zs.pop(k) - below)
            k0 = pl.multiple_of(blk * TK, TK)
            results[k] = _dot(a.astype(_BF16), v_ref[pl.ds(k0, kn), heads[h]])

        order = range(len(items))
        return [lambda k=k, f=f: f(k) for f in (logits, softplus, suffix, weigh) for k in order]

    def live(carries):
        low = carries[0]
        for c in carries[1:]:
            low = jnp.minimum(low, c)
        return (jnp.min(low) < -_EXP_ZERO_CUTOFF).astype(jnp.int32)

    def window_steps(t, holder):
        blk = blocks[t]
        prev = jnp.maximum(blk - 1, 0)
        no_prev = jnp.where(blk > 0, 0.0, _HUGE_CARRY)
        d_items, d_tots, d_pvs, diag = [], {}, {}, {}
        for h, hs in enumerate(heads):
            d_items.append((lambda hs=hs: q_ref[t, :half, hs], h, blk, half, causal_top, None))
            d_items.append((lambda hs=hs: q_ref[t, half:, hs], h, blk, TK, causal_bottom, None))

        def diag_carry(h):
            if h not in diag:
                diag[h] = jnp.concatenate([d_tots[2 * h], d_tots[2 * h + 1]], axis=0) + no_prev
            return diag[h]

        p_items = [(lambda hs=hs: q_ref[t, :, hs], h, prev, TK, None,
                    lambda h=h: diag_carry(h)) for h, hs in enumerate(heads)]
        p_tots, p_pvs = {}, {}
        d_steps = sweep_steps(d_items, d_tots, d_pvs)
        p_steps = sweep_steps(p_items, p_tots, p_pvs)
        nd, npv = len(d_items), len(p_items)

        def carries():
            holder["carry"] = [diag_carry(h) + p_tots[h] for h in range(B_HEADS)]
            holder["go"] = live(holder["carry"])

        def finish():
            holder["acc"] = [
                jnp.concatenate([d_pvs[2 * h], d_pvs[2 * h + 1]], axis=0) + p_pvs[h]
                for h in range(B_HEADS)]

        ordered = []
        per = B_HEADS // HEAD_GROUPS
        for grp in range(HEAD_GROUPS):
            hs_ = range(grp * per, (grp + 1) * per)
            for stage in range(4):
                for h in hs_:
                    ordered += d_steps[stage * nd + 2 * h:stage * nd + 2 * h + 2]
                for h in hs_:
                    ordered.append(p_steps[stage * npv + h])
                if stage == 2 and grp == HEAD_GROUPS - 1:
                    ordered.append(carries)
        return ordered + [finish]

    def finish_sweep(t, holder):
        def full_block(j, carry, acc):
            tots, pvs = {}, {}
            items = [(lambda hs=hs: q_ref[t, :, hs], h, j, TK, None,
                      lambda h=h: carry[h]) for h, hs in enumerate(heads)]
            for f in sweep_steps(items, tots, pvs):
                f()
            return ([carry[h] + tots[h] for h in range(B_HEADS)],
                    [acc[h] + pvs[h] for h in range(B_HEADS)])

        def cond(loop):
            return jnp.logical_and(loop[0] >= 0, loop[1] > 0)

        def body(loop):
            carry, acc = full_block(loop[0], loop[2], loop[3])
            return loop[0] - 1, live(carry), carry, acc

        out = lax.while_loop(cond, body, (blocks[t] - 2, holder["go"], holder["carry"], holder["acc"]))
        holder["acc"] = out[3]

    def kv_steps(t):
        r0 = pl.multiple_of(blocks[t] * TQ, TQ)

        def norm():
            hn_ref[t] = _rms(out_ref[rows[t], :], kvg_ref[...]).astype(_BF16)

        def chunk(c):
            kv = _dot(hn_ref[t], w_kv_ref[:, c * COLS:(c + 1) * COLS])
            cs = _COL_SLICES[c % (D_MODEL // COLS)]
            if c < D_MODEL // COLS:
                kt_ref[blocks[t], cs, :] = kv.T.astype(_BF16)
            else:
                v_ref[pl.ds(r0, TQ), cs] = kv.astype(_BF16)

        return [norm] + [lambda c=c: chunk(c) for c in range(2 * D_MODEL // COLS)]

    def pre_steps(t, j):
        def norm():
            hn_ref[t] = _rms(out_ref[rows[t], :], ng_ref[j]).astype(_BF16)

        def query(c):
            cs = _COL_SLICES[c]
            q_ref[t, :, cs] = (_dot(hn_ref[t], w_in_ref[j, :, cs]) * _Q_SCALE).astype(_BF16)

        def gate(c):
            cs = _COL_SLICES[c]
            g = _dot(hn_ref[t], w_in_ref[j, :, D_MODEL + c * COLS:D_MODEL + (c + 1) * COLS])
            sg_ref[t, :, cs] = _silu(g).astype(_BF16)

        n_c = len(_COL_SLICES)
        return ([norm] + [lambda c=c: query(c) for c in range(n_c)]
                + [lambda c=c: gate(c) for c in range(n_c)])

    def post_steps(t, j, holder):
        def gated():
            for hh, hs in enumerate(heads):
                sg_ref[t, :, hs] = holder["acc"][hh].astype(_BF16) * sg_ref[t, :, hs]

        def mix(c):
            cs = _COL_SLICES[c]
            h1 = out_ref[rows[t], cs] + _dot(sg_ref[t], w_out_ref[j, :, cs])
            out_ref[rows[t], cs] = h1
            hn_ref[t, :, cs] = h1.astype(_BF16)

        def ple(c):
            cs = _COL_SLICES[c]
            pg = _dot(hn_ref[t], gate_w_ref[j, :, cs])
            pe = _dot(p_ref[j, rows[t], :].astype(_BF16), ple_w_ref[j, :, cs])
            out_ref[rows[t], cs] = out_ref[rows[t], cs] + jax.nn.sigmoid(pg) * pe

        n_c = len(_COL_SLICES)
        return ([gated] + [lambda c=c: mix(c) for c in range(n_c)]
                + [lambda c=c: ple(c) for c in range(n_c)])

    def program(sweep):
        x, y = 0, 1

        def settle(t, holder, more):
            if sweep:
                finish_sweep(t, holder)
                return more
            return jnp.maximum(more, jnp.where(blocks[t] >= 2, holder["go"], 0))

        def layer(j, more):
            att = [{} for _ in range(TILES)]
            for f in pre_steps(x, j):
                f()
            _interleave(window_steps(x, att[x]), pre_steps(y, j))
            more = settle(x, att[x], more)
            _interleave(window_steps(y, att[y]), post_steps(x, j, att[x]))
            more = settle(y, att[y], more)
            for f in post_steps(y, j, att[y]):
                f()
            return more

        out_ref[...] = h_ref[...]
        for f in kv_steps(x) + kv_steps(y):
            f()
        more = jnp.int32(0)
        for j in range(n_b):
            more = layer(j, more)
        for t in range(TILES):
            out_ref[rows[t], :] = _rms(out_ref[rows[t], :], fg_ref[...])
        return more

    redo = program(sweep=False)

    @pl.when(redo > 0)
    def _():
        program(sweep=True)


def _b_layers(h3, p2, norm_g, w_in, w_out, gate_w, ple_w, final_g, kv_g, w_kv):
    b, s, d = h3.shape
    tm = TILES * TQ
    n_steps = s // tm
    n_b = DEPTH - N_A

    def const(shape):
        nd = len(shape)
        return pl.BlockSpec(shape, lambda bb, ii, _nd=nd: (0,) * _nd,
                            pipeline_mode=pl.Buffered(1))

    def upper_half(shape):
        nd = len(shape)
        return pl.BlockSpec(shape, lambda bb, ii, _nd=nd: (N_A // n_b,) + (0,) * (_nd - 1),
                            pipeline_mode=pl.Buffered(1))

    tile = lambda: pl.BlockSpec((None, tm, d), lambda bb, ii: (bb, ii, 0))
    return pl.pallas_call(
        _b_layers_kernel,
        grid=(b, n_steps),
        in_specs=[
            tile(),
            pl.BlockSpec((n_b, tm, PLE_DIM), lambda bb, ii: (N_A // n_b, bb * n_steps + ii, 0)),
            upper_half((n_b, 1, d)),
            const((n_b, d, 2 * d)),
            const((n_b, d, d)),
            upper_half((n_b, d, d)),
            upper_half((n_b, PLE_DIM, d)),
            const((1, d)),
            const((1, d)),
            const((d, 2 * d)),
        ],
        out_specs=tile(),
        out_shape=jax.ShapeDtypeStruct((b, s, d), _F32),
        scratch_shapes=[
            pltpu.VMEM((s // TK, d, TK), _BF16),
            pltpu.VMEM((s, d), _BF16),
            pltpu.VMEM((TILES, TQ, d), _BF16),
            pltpu.VMEM((TILES, TQ, d), _BF16),
            pltpu.VMEM((TILES, TQ, d), _BF16),
        ],
        compiler_params=pltpu.CompilerParams(
            dimension_semantics=("arbitrary", "arbitrary"),
            vmem_limit_bytes=VMEM_LIMIT),
        name="b_layers",
    )(h3, p2, norm_g, w_in, w_out, gate_w, ple_w, final_g, kv_g, w_kv)


def kernel(x, p, norm_g, a_w_in, a_ln_g, a_ln_b, a_w_s, a_b_s, a_w_out, kv_norm_g, w_kv, b_w_in, b_w_out, ple_w, ple_gate_w, final_g):
    b, s, d = x.shape
    t = b * s
    n_b = DEPTH - N_A
    bf = lambda w: w.astype(_BF16)

    h = x.reshape(t, d)
    p2 = p.reshape(DEPTH, t, PLE_DIM)
    norm_g3 = norm_g.reshape(DEPTH, 1, d)
    ple_wb = bf(ple_w)
    small = (norm_g3, a_ln_g.reshape(N_A, 1, -1), a_ln_b.reshape(N_A, 1, -1), a_w_s,
             jnp.swapaxes(a_b_s, 1, 2), ple_wb)

    n_steps = t // TM_A
    slab = lambda rows: rows // n_steps
    casts = (
        (a_w_in.reshape(N_A * d, -1), slab(d), n_steps),
        (a_w_out.reshape(N_A * A_WIDTH, d), slab(A_WIDTH), n_steps),
        (b_w_in.reshape(n_b * d, -1), slab(n_b * d), 0),
        (b_w_out.reshape(n_b * d, d), slab(n_b * d), 0),
        (w_kv, slab(d), 0),
        (ple_gate_w.reshape(DEPTH * d, d), slab(DEPTH * d), 0),
    )
    h, (w_in1, w_out1, b_w_in_b, b_w_out_b, w_kv_b, gate_b) = _a_layer(
        h, p2, 0, *small, bf(a_w_in[0])[None], bf(a_w_out[0])[None], bf(ple_gate_w[0])[None], 0,
        casts=casts)
    gate_b = gate_b.reshape(DEPTH, d, d)
    for i in range(1, N_A):
        h, _ = _a_layer(h, p2, i, *small, w_in1[None], w_out1[None], gate_b, i)

    return _b_layers(h.reshape(b, s, d), p2, norm_g3, b_w_in_b.reshape(n_b, d, -1),
                     b_w_out_b.reshape(n_b, d, d), gate_b, ple_wb,
                     final_g.reshape(1, d), kv_norm_g.reshape(1, d), w_kv_b)
```

```python
import functools

import jax
import jax.numpy as jnp
import numpy as np
from jax import lax
from jax.experimental import pallas as pl
from jax.experimental.pallas import tpu as pltpu

D_MODEL = 1024
DEPTH = 4
N_A = DEPTH // 2
CHUNK = 64
GM_BLOCK = 128
A_WIDTH = 2 * D_MODEL
A_GROUPS = 8
A_GROUP_CH = A_WIDTH // A_GROUPS
B_HEADS = 8
B_HEAD_DIM = D_MODEL // B_HEADS
PLE_DIM = 256
EPS = 1e-6

COLS = 256
TM_A = 512
TQ = 256
TK = 256
TILES = 2
HEAD_GROUPS = 4
VMEM_LIMIT = 60 * 1024 * 1024

_SQRT_HALF = float(np.sqrt(0.5))
_Q_SCALE = float(1.0 / np.sqrt(B_HEAD_DIM))
_EXP_ZERO_CUTOFF = -105.0
_HUGE_CARRY = 1e30
_NEG_LOG2E = float(-1.0 / np.log(2.0))
_F32 = jnp.float32
_BF16 = jnp.bfloat16


def _rms(x, g):
    ms = jnp.mean(x * x, axis=-1, keepdims=True)
    return x * lax.rsqrt(ms + EPS) * g


def _gelu(x):
    return 0.5 * x * (1.0 + lax.erf(x * _SQRT_HALF))


def _silu(x):
    return x * jax.nn.sigmoid(x)


def _dot(a, b):
    return jnp.dot(a, b, preferred_element_type=_F32)


_HEAD_SLICES = [slice(hh * B_HEAD_DIM, (hh + 1) * B_HEAD_DIM) for hh in range(B_HEADS)]
_COL_SLICES = [slice(c * COLS, (c + 1) * COLS) for c in range(D_MODEL // COLS)]


def _interleave(primary, filler):
    done = 0
    for k, step in enumerate(primary):
        step()
        upto = (k + 1) * len(filler) // len(primary)
        for f in filler[done:upto]:
            f()
        done = upto


def _a_layer_kernel(n_cast, h_ref, p_ref, ng_ref, w_in_ref, lng_ref, lnb_ref, ws_ref,
                    bst_ref, w_out_ref, gate_w_ref, ple_w_ref, *refs):
    cast_in, out_ref, cast_out = refs[:n_cast], refs[n_cast], refs[n_cast + 1:2 * n_cast + 1]
    hn_ref, v_ref, vn_ref, ug_ref, y_ref, h1b_ref = refs[2 * n_cast + 1:]
    tm = h_ref.shape[0]
    for src, dst in zip(cast_in, cast_out):
        dst[...] = src[...].astype(_BF16)

    n_grp = tm // GM_BLOCK
    chunks = [slice(c * COLS, (c + 1) * COLS) for c in range(A_WIDTH // COLS)]

    hn_ref[...] = _rms(h_ref[...], ng_ref[...]).astype(_BF16)

    stat = {"s1": jnp.zeros((tm, 1), _F32)}
    for c, cs in enumerate(chunks):
        vc = _gelu(_dot(hn_ref[...], w_in_ref[:, A_WIDTH + c * COLS:A_WIDTH + (c + 1) * COLS]))
        v_ref[:, cs] = vc
        stat["s1"] = stat["s1"] + jnp.sum(vc, axis=-1, keepdims=True)

    def mean():
        stat["mu"] = stat["s1"] * (1.0 / A_WIDTH)
        stat["s2"] = jnp.zeros((tm, 1), _F32)

    def center(cs):
        d = v_ref[:, cs] - stat["mu"]
        stat["s2"] = stat["s2"] + jnp.sum(d * d, axis=-1, keepdims=True)

    def scale():
        stat["rstd"] = lax.rsqrt(stat["s2"] * (1.0 / A_WIDTH) + EPS)

    def normalize(cs):
        vn = (v_ref[:, cs] - stat["mu"]) * stat["rstd"] * lng_ref[:, cs] + lnb_ref[:, cs]
        vn_ref[:, cs] = vn.astype(_BF16)

    def u_gate(g):
        cs = slice(g * A_GROUP_CH, (g + 1) * A_GROUP_CH)
        u = _gelu(_dot(hn_ref[...], w_in_ref[:, cs]))
        gate = _dot(hn_ref[...], w_in_ref[:, 2 * A_WIDTH + g * A_GROUP_CH:2 * A_WIDTH + (g + 1) * A_GROUP_CH])
        ug_ref[:, cs] = u * _silu(gate)

    layernorm = ([mean] + [lambda cs=cs: center(cs) for cs in chunks] + [scale]
                 + [lambda cs=cs: normalize(cs) for cs in chunks])
    _interleave([lambda g=g: u_gate(g) for g in range(A_GROUPS)], layernorm)

    t_chunk = lax.broadcasted_iota(jnp.int32, (GM_BLOCK, GM_BLOCK), 0) // CHUNK
    s_chunk = lax.broadcasted_iota(jnp.int32, (GM_BLOCK, GM_BLOCK), 1) // CHUNK
    mask = s_chunk <= t_chunk

    for g in range(A_GROUPS):
        cs = slice(g * A_GROUP_CH, (g + 1) * A_GROUP_CH)
        wm = jnp.where(mask, ws_ref[g], 0.0).astype(_BF16)
        b_col = bst_ref[:, g:g + 1]
        sv = jnp.concatenate(
            [_dot(wm, vn_ref[n * GM_BLOCK:(n + 1) * GM_BLOCK, cs]) + b_col
             for n in range(n_grp)], axis=0)
        y_ref[:, cs] = (ug_ref[:, cs] * sv).astype(_BF16)

    for cs in _COL_SLICES:
        h1 = h_ref[:, cs] + _dot(y_ref[...], w_out_ref[:, cs])
        out_ref[:, cs] = h1
        h1b_ref[:, cs] = h1.astype(_BF16)
    pb = p_ref[...].astype(_BF16)
    for cs in _COL_SLICES:
        pg = _dot(h1b_ref[...], gate_w_ref[:, cs])
        pe = _dot(pb, ple_w_ref[:, cs])
        out_ref[:, cs] = out_ref[:, cs] + jax.nn.sigmoid(pg) * pe


def _a_layer(h, p2, layer, norm_g, ln_g, ln_b, w_s, b_s_t, ple_w, w_in, w_out, gate_w, gate_layer, casts=()):
    t = h.shape[0]
    tm = TM_A
    n_steps = t // tm

    def param(index, *shape):
        return pl.BlockSpec((None,) + shape, lambda i: (index,) + (0,) * len(shape),
                            pipeline_mode=pl.Buffered(1))

    cast_in, cast_out, cast_shapes = [], [], []
    for src, slab, first in casts:
        cols = src.shape[1]
        cast_in.append(pl.BlockSpec((slab, cols), lambda i, _f=first: (_f + i, 0)))
        cast_out.append(pl.BlockSpec((slab, cols), lambda i: (i, 0)))
        cast_shapes.append(jax.ShapeDtypeStruct((slab * n_steps, cols), _BF16))

    out = pl.pallas_call(
        functools.partial(_a_layer_kernel, len(casts)),
        grid=(n_steps,),
        in_specs=[
            pl.BlockSpec((tm, D_MODEL), lambda i: (i, 0)),
            pl.BlockSpec((None, tm, PLE_DIM), lambda i: (layer, i, 0)),
            param(layer, 1, D_MODEL),
            param(0, D_MODEL, 3 * A_WIDTH),
            param(layer, 1, A_WIDTH),
            param(layer, 1, A_WIDTH),
            param(layer, A_GROUPS, GM_BLOCK, GM_BLOCK),
            param(layer, GM_BLOCK, A_GROUPS),
            param(0, A_WIDTH, D_MODEL),
            param(gate_layer, D_MODEL, D_MODEL),
            param(layer, PLE_DIM, D_MODEL),
        ] + cast_in,
        out_specs=[pl.BlockSpec((tm, D_MODEL), lambda i: (i, 0))] + cast_out,
        out_shape=[jax.ShapeDtypeStruct((t, D_MODEL), _F32)] + cast_shapes,
        scratch_shapes=[
            pltpu.VMEM((tm, D_MODEL), _BF16),
            pltpu.VMEM((tm, A_WIDTH), _F32),
            pltpu.VMEM((tm, A_WIDTH), _BF16),
            pltpu.VMEM((tm, A_WIDTH), _F32),
            pltpu.VMEM((tm, A_WIDTH), _BF16),
            pltpu.VMEM((tm, D_MODEL), _BF16),
        ],
        compiler_params=pltpu.CompilerParams(
            dimension_semantics=("arbitrary",), vmem_limit_bytes=VMEM_LIMIT),
        name="a_layer",
    )(h, p2, norm_g, w_in, ln_g, ln_b, w_s, b_s_t, w_out, gate_w, ple_w, *[c[0] for c in casts])
    return out[0], out[1:]


def _b_layers_kernel(h_ref, p_ref, ng_ref, w_in_ref, w_out_ref, gate_w_ref,
                     ple_w_ref, fg_ref, kvg_ref, w_kv_ref, out_ref,
                     kt_ref, v_ref, hn_ref, q_ref, sg_ref):
    step = pl.program_id(1)
    n_b = DEPTH - N_A
    heads = _HEAD_SLICES
    rows = [slice(t * TQ, (t + 1) * TQ) for t in range(TILES)]
    blocks = [step * TILES + t for t in range(TILES)]

    half = TK // 2
    row = lax.broadcasted_iota(jnp.int32, (TK, TK), 0)
    col = lax.broadcasted_iota(jnp.int32, (TK, TK), 1)
    upper = {TK: jnp.where(row > col, 1.0, 0.0).astype(_BF16)}
    upper[half] = upper[TK][:half, :half]
    causal_top = (col < row)[:half, :half]
    causal_bottom = (col < row)[half:, :]

    def sweep_steps(items, tots, results):
        zs, nls, excl = {}, {}, {}

        def logits(k):
            q, h, blk, kn, mask = items[k][:5]
            z = _dot(q(), kt_ref[blk, heads[h], :kn])
            zs[k] = z if mask is None else jnp.where(mask, z, -_HUGE_CARRY)

        def softplus(k):
            z = zs[k]
            nls[k] = jnp.maximum(z, 0.0) + jnp.log(1.0 + jnp.exp2(jnp.abs(z) * _NEG_LOG2E))

        def suffix(k):
            excl[k] = _dot(nls[k].astype(_BF16), upper[items[k][3]])
            tots[k] = excl[k][:, 0:1] + nls[k][:, 0:1]

        def weigh(k):
            _, h, blk, kn, _, carry = items[k]
            below = nls.pop(k) + excl.pop(k)
            if carry is not None:
                below = below + carry()
            a = jnp.exp(zs.pop(k) - below)
            k0 = pl.multiple_of(blk * TK, TK)
            results[k] = _dot(a.astype(_BF16), v_ref[pl.ds(k0, kn), heads[h]])

        order = range(len(items))
        return [lambda k=k, f=f: f(k) for f in (logits, softplus, suffix, weigh) for k in order]

    def live(carries):
        low = carries[0]
        for c in carries[1:]:
            low = jnp.minimum(low, c)
        return (jnp.min(low) < -_EXP_ZERO_CUTOFF).astype(jnp.int32)

    def window_steps(t, holder):
        blk = blocks[t]
        prev = jnp.maximum(blk - 1, 0)
        no_prev = jnp.where(blk > 0, 0.0, _HUGE_CARRY)
        d_items, d_tots, d_pvs, diag = [], {}, {}, {}
        for h, hs in enumerate(heads):
            d_items.append((lambda hs=hs: q_ref[t, :half, hs], h, blk, half, causal_top, None))
            d_items.append((lambda hs=hs: q_ref[t, half:, hs], h, blk, TK, causal_bottom, None))

        def diag_carry(h):
            if h not in diag:
                diag[h] = jnp.concatenate([d_tots[2 * h], d_tots[2 * h + 1]], axis=0) + no_prev
            return diag[h]

        p_items = [(lambda hs=hs: q_ref[t, :, hs], h, prev, TK, None,
                    lambda h=h: diag_carry(h)) for h, hs in enumerate(heads)]
        p_tots, p_pvs = {}, {}
        d_steps = sweep_steps(d_items, d_tots, d_pvs)
        p_steps = sweep_steps(p_items, p_tots, p_pvs)
        nd, npv = len(d_items), len(p_items)

        def carries():
            holder["carry"] = [diag_carry(h) + p_tots[h] for h in range(B_HEADS)]
            holder["go"] = live(holder["carry"])

        def finish():
            holder["acc"] = [
                jnp.concatenate([d_pvs[2 * h], d_pvs[2 * h + 1]], axis=0) + p_pvs[h]
                for h in range(B_HEADS)]

        ordered = []
        per = B_HEADS // HEAD_GROUPS
        for grp in range(HEAD_GROUPS):
            hs_ = range(grp * per, (grp + 1) * per)
            for stage in range(4):
                for h in hs_:
                    ordered += d_steps[stage * nd + 2 * h:stage * nd + 2 * h + 2]
                for h in hs_:
                    ordered.append(p_steps[stage * npv + h])
                if stage == 2 and grp == HEAD_GROUPS - 1:
                    ordered.append(carries)
        return ordered + [finish]

    def finish_sweep(t, holder):
        def full_block(j, carry, acc):
            tots, pvs = {}, {}
            items = [(lambda hs=hs: q_ref[t, :, hs], h, j, TK, None,
                      lambda h=h: carry[h]) for h, hs in enumerate(heads)]
            for f in sweep_steps(items, tots, pvs):
                f()
            return ([carry[h] + tots[h] for h in range(B_HEADS)],
                    [acc[h] + pvs[h] for h in range(B_HEADS)])

        def cond(loop):
            return jnp.logical_and(loop[0] >= 0, loop[1] > 0)

        def body(loop):
            carry, acc = full_block(loop[0], loop[2], loop[3])
            return loop[0] - 1, live(carry), carry, acc

        out = lax.while_loop(cond, body, (blocks[t] - 2, holder["go"], holder["carry"], holder["acc"]))
        holder["acc"] = out[3]

    def kv_steps(t):
        r0 = pl.multiple_of(blocks[t] * TQ, TQ)

        def norm():
            hn_ref[t] = _rms(out_ref[rows[t], :], kvg_ref[...]).astype(_BF16)

        def chunk(c):
            kv = _dot(hn_ref[t], w_kv_ref[:, c * COLS:(c + 1) * COLS])
            cs = _COL_SLICES[c % (D_MODEL // COLS)]
            if c < D_MODEL // COLS:
                kt_ref[blocks[t], cs, :] = kv.T.astype(_BF16)
            else:
                v_ref[pl.ds(r0, TQ), cs] = kv.astype(_BF16)

        return [norm] + [lambda c=c: chunk(c) for c in range(2 * D_MODEL // COLS)]

    def pre_steps(t, j):
        def norm():
            hn_ref[t] = _rms(out_ref[rows[t], :], ng_ref[j]).astype(_BF16)

        def query(c):
            cs = _COL_SLICES[c]
            q_ref[t, :, cs] = (_dot(hn_ref[t], w_in_ref[j, :, cs]) * _Q_SCALE).astype(_BF16)

        def gate(c):
            cs = _COL_SLICES[c]
            g = _dot(hn_ref[t], w_in_ref[j, :, D_MODEL + c * COLS:D_MODEL + (c + 1) * COLS])
            sg_ref[t, :, cs] = _silu(g).astype(_BF16)

        n_c = len(_COL_SLICES)
        return ([norm] + [lambda c=c: query(c) for c in range(n_c)]
                + [lambda c=c: gate(c) for c in range(n_c)])

    def post_steps(t, j, holder):
        def gated():
            for hh, hs in enumerate(heads):
                sg_ref[t, :, hs] = holder["acc"][hh].astype(_BF16) * sg_ref[t, :, hs]

        def mix(c):
            cs = _COL_SLICES[c]
            h1 = out_ref[rows[t], cs] + _dot(sg_ref[t], w_out_ref[j, :, cs])
            out_ref[rows[t], cs] = h1
            hn_ref[t, :, cs] = h1.astype(_BF16)

        def ple(c):
            cs = _COL_SLICES[c]
            pg = _dot(hn_ref[t], gate_w_ref[j, :, cs])
            pe = _dot(p_ref[j, rows[t], :].astype(_BF16), ple_w_ref[j, :, cs])
            out_ref[rows[t], cs] = out_ref[rows[t], cs] + jax.nn.sigmoid(pg) * pe

        n_c = len(_COL_SLICES)
        return ([gated] + [lambda c=c: mix(c) for c in range(n_c)]
                + [lambda c=c: ple(c) for c in range(n_c)])

    def program(sweep):
        x, y = 0, 1

        def settle(t, holder, more):
            if sweep:
                finish_sweep(t, holder)
                return more
            return jnp.maximum(more, jnp.where(blocks[t] >= 2, holder["go"], 0))

        def layer(j, more):
            att = [{} for _ in range(TILES)]
            _interleave(window_steps(x, att[x]), pre_steps(y, j))
            more = settle(x, att[x], more)
            ahead = pre_steps(x, j + 1) if j + 1 < n_b else []
            _interleave(window_steps(y, att[y]), post_steps(x, j, att[x]) + ahead)
            more = settle(y, att[y], more)
            for f in post_steps(y, j, att[y]):
                f()
            return more

        out_ref[...] = h_ref[...]
        for f in kv_steps(x) + kv_steps(y) + pre_steps(x, 0):
            f()
        more = jnp.int32(0)
        for j in range(n_b):
            more = layer(j, more)
        for t in range(TILES):
            out_ref[rows[t], :] = _rms(out_ref[rows[t], :], fg_ref[...])
        return more

    redo = program(sweep=False)

    @pl.when(redo > 0)
    def _():
        program(sweep=True)


def _b_layers(h3, p2, norm_g, w_in, w_out, gate_w, ple_w, final_g, kv_g, w_kv):
    b, s, d = h3.shape
    tm = TILES * TQ
    n_steps = s // tm
    n_b = DEPTH - N_A

    def const(shape):
        nd = len(shape)
        return pl.BlockSpec(shape, lambda bb, ii, _nd=nd: (0,) * _nd,
                            pipeline_mode=pl.Buffered(1))

    def upper_half(shape):
        nd = len(shape)
        return pl.BlockSpec(shape, lambda bb, ii, _nd=nd: (N_A // n_b,) + (0,) * (_nd - 1),
                            pipeline_mode=pl.Buffered(1))

    tile = lambda: pl.BlockSpec((None, tm, d), lambda bb, ii: (bb, ii, 0))
    return pl.pallas_call(
        _b_layers_kernel,
        grid=(b, n_steps),
        in_specs=[
            tile(),
            pl.BlockSpec((n_b, tm, PLE_DIM), lambda bb, ii: (N_A // n_b, bb * n_steps + ii, 0)),
            upper_half((n_b, 1, d)),
            const((n_b, d, 2 * d)),
            const((n_b, d, d)),
            upper_half((n_b, d, d)),
            upper_half((n_b, PLE_DIM, d)),
            const((1, d)),
            const((1, d)),
            const((d, 2 * d)),
        ],
        out_specs=tile(),
        out_shape=jax.ShapeDtypeStruct((b, s, d), _F32),
        scratch_shapes=[
            pltpu.VMEM((s // TK, d, TK), _BF16),
            pltpu.VMEM((s, d), _BF16),
            pltpu.VMEM((TILES, TQ, d), _BF16),
            pltpu.VMEM((TILES, TQ, d), _BF16),
            pltpu.VMEM((TILES, TQ, d), _BF16),
        ],
        compiler_params=pltpu.CompilerParams(
            dimension_semantics=("arbitrary", "arbitrary"),
            vmem_limit_bytes=VMEM_LIMIT),
        name="b_layers",
    )(h3, p2, norm_g, w_in, w_out, gate_w, ple_w, final_g, kv_g, w_kv)


def kernel(x, p, norm_g, a_w_in, a_ln_g, a_ln_b, a_w_s, a_b_s, a_w_out, kv_norm_g, w_kv, b_w_in, b_w_out, ple_w, ple_gate_w, final_g):
    b, s, d = x.shape
    t = b * s
    n_b = DEPTH - N_A
    bf = lambda w: w.astype(_BF16)

    h = x.reshape(t, d)
    p2 = p.reshape(DEPTH, t, PLE_DIM)
    norm_g3 = norm_g.reshape(DEPTH, 1, d)
    ple_wb = bf(ple_w)
    small = (norm_g3, a_ln_g.reshape(N_A, 1, -1), a_ln_b.reshape(N_A, 1, -1), a_w_s,
             jnp.swapaxes(a_b_s, 1, 2), ple_wb)

    n_steps = t // TM_A
    slab = lambda rows: rows // n_steps
    casts = (
        (a_w_in.reshape(N_A * d, -1), slab(d), n_steps),
        (a_w_out.reshape(N_A * A_WIDTH, d), slab(A_WIDTH), n_steps),
        (b_w_in.reshape(n_b * d, -1), slab(n_b * d), 0),
        (b_w_out.reshape(n_b * d, d), slab(n_b * d), 0),
        (w_kv, slab(d), 0),
        (ple_gate_w.reshape(DEPTH * d, d), slab(DEPTH * d), 0),
    )
    h, (w_in1, w_out1, b_w_in_b, b_w_out_b, w_kv_b, gate_b) = _a_layer(
        h, p2, 0, *small, bf(a_w_in[0])[None], bf(a_w_out[0])[None], bf(ple_gate_w[0])[None], 0,
        casts=casts)
    gate_b = gate_b.reshape(DEPTH, d, d)
    for i in range(1, N_A):
        h, _ = _a_layer(h, p2, i, *small, w_in1[None], w_out1[None], gate_b, i)

    return _b_layers(h.reshape(b, s, d), p2, norm_g3, b_w_in_b.reshape(n_b, d, -1),
                     b_w_out_b.reshape(n_b, d, d), gate_b, ple_wb,
                     final_g.reshape(1, d), kv_norm_g.reshape(1, d), w_kv_b)
```
